```python
import math
import jax, jax.numpy as jnp
from jax import lax
import numpy as np


D_MODEL = 2048
BATCH = 1
SEQ = 16384
DEPTH = 1
DEC_BATCH = 32
DEC_SEQ = 4
PAST_LEN = 16384
PAGE_SIZE = 128

HK = 16
HV = 32
DK = 128
DV = 128
CONV_K = 4
CHUNK = 64
HB = 8
DH = 128
PATTERNS = ((128, 1), (512, 4), (2048, 16))
NG = len(PATTERNS)
QBLK = 128
EPS = 1e-6
NEG_INF = -1e30

QK_DIM = HK * DK
V_DIM = HV * DV
CONV_DIM = 2 * QK_DIM + V_DIM
B_DIM = NG * HB * DH
B_OUT = HB * DH
SPLIT_SIZES = (QK_DIM, QK_DIM, V_DIM, V_DIM, HV, HV, B_DIM, B_DIM, B_DIM, B_OUT, D_MODEL, D_MODEL)
IN_DIM = sum(SPLIT_SIZES)
SPLIT_IDX = tuple(int(s) for s in np.cumsum(SPLIT_SIZES)[:-1])

kernel_name = 'hybrid_gdn_dilated_swa_step'


def rmsnorm(x, w):
    xf = x.astype(jnp.float32)
    r = lax.rsqrt(jnp.mean(xf * xf, axis=-1, keepdims=True) + EPS)
    return (xf * r * w.astype(jnp.float32)).astype(x.dtype)


def l2norm(x):
    xf = x.astype(jnp.float32)
    return xf * lax.rsqrt(jnp.sum(xf * xf, axis=-1, keepdims=True) + EPS)


def gated_rmsnorm(o, z, w):
    of = o.astype(jnp.float32)
    of = of * lax.rsqrt(jnp.mean(of * of, axis=-1, keepdims=True) + EPS) * w.astype(jnp.float32)
    return (of * jax.nn.silu(z.astype(jnp.float32))).astype(z.dtype)


def causal_conv_silu(x, buf, w):
    t = x.shape[1]
    xp = jnp.concatenate([buf.astype(x.dtype), x], axis=1)
    y = xp[:, 0:t] * w[0]
    for i in range(1, w.shape[0]):
        y = y + xp[:, i:i + t] * w[i]
    return jax.nn.silu(y), xp[:, t:]


def gated_delta_chunked(q, k, v, beta, g, s0):
    bn, t, h = q.shape[:3]
    n = t // CHUNK

    def blocks(a):
        a = a.reshape((bn, n, CHUNK, h) + a.shape[3:])
        return jnp.moveaxis(a, 3, 1)

    qc, kc, vc, bc, gc = blocks(q), blocks(k), blocks(v), blocks(beta), blocks(g)
    G = jnp.cumsum(gc, axis=-1)
    causal = jnp.tril(jnp.ones((CHUNK, CHUNK), bool))
    strict = jnp.tril(jnp.ones((CHUNK, CHUNK), bool), -1)
    decay = jnp.exp(jnp.where(causal, G[..., :, None] - G[..., None, :], -jnp.inf))
    kb = kc * bc[..., None]
    L = jnp.where(strict, jnp.einsum('bhnid,bhnjd->bhnij', kb, kc) * decay, 0.0)
    A = L + jnp.eye(CHUNK, dtype=L.dtype)
    w = lax.linalg.triangular_solve(A, kb * jnp.exp(G)[..., None], left_side=True, lower=True, unit_diagonal=True)
    u = lax.linalg.triangular_solve(A, vc * bc[..., None], left_side=True, lower=True, unit_diagonal=True)
    intra = jnp.einsum('bhnid,bhnjd->bhnij', qc, kc) * decay
    qg = qc * jnp.exp(G)[..., None]
    kd = kc * jnp.exp(G[..., -1:] - G)[..., None]
    gl = jnp.exp(G[..., -1])

    def step(S, xs):
        w_c, u_c, qg_c, kd_c, a_c, gl_c = xs
        vn = u_c - jnp.einsum('bhcd,bhde->bhce', w_c, S)
        o = jnp.einsum('bhcd,bhde->bhce', qg_c, S) + jnp.einsum('bhij,bhje->bhie', a_c, vn)
        S = S * gl_c[..., None, None] + jnp.einsum('bhcd,bhce->bhde', kd_c, vn)
        return S, o

    xs = tuple(jnp.moveaxis(a, 2, 0) for a in (w, u, qg, kd, intra, gl))
    s_new, o = lax.scan(step, s0, xs)
    o = jnp.transpose(o, (1, 0, 3, 2, 4)).reshape(bn, t, h, o.shape[-1])
    return o, s_new


def gated_delta_recurrent(q, k, v, beta, g, s0):
    def step(S, xs):
        q_t, k_t, v_t, b_t, g_t = xs
        S = S * jnp.exp(g_t)[..., None, None]
        vn = b_t[..., None] * (v_t - jnp.einsum('bhd,bhde->bhe', k_t, S))
        S = S + k_t[..., :, None] * vn[..., None, :]
        return S, jnp.einsum('bhd,bhde->bhe', q_t, S)

    xs = tuple(jnp.moveaxis(a, 1, 0) for a in (q, k, v, beta, g))
    s_new, o = lax.scan(step, s0, xs)
    return jnp.moveaxis(o, 0, 1), s_new


def dilated_attn(q, k, v, q_idx, dil, win):
    n_keys = win // dil + 1
    idx = q_idx[:, None] - dil * jnp.arange(n_keys)[None, :]
    valid = idx >= 0
    idx = jnp.maximum(idx, 0)
    kg = k[:, idx]
    vg = v[:, idx]
    s = jnp.einsum('bqhd,bqjhd->bhqj', q, kg).astype(jnp.float32) * (DH ** -0.5)
    s = jnp.where(valid[None, None], s, NEG_INF)
    m = jnp.max(s, axis=-1, keepdims=True)
    p = jnp.exp(s - m)
    l = jnp.sum(p, axis=-1, keepdims=True)
    o = jnp.einsum('bhqj,bqjhd->bqhd', (p / l).astype(v.dtype), vg)
    lse = (m + jnp.log(l))[..., 0]
    return o, lse


def mix_groups(outs, lses):
    wgt = jax.nn.softmax(jnp.stack(lses, axis=0), axis=0)
    wgt = jnp.transpose(wgt, (0, 1, 3, 2))[..., None]
    return jnp.sum(wgt.astype(outs[0].dtype) * jnp.stack(outs, axis=0), axis=0)


def dilated_prompt(qb, kb, vb):
    bn, t = qb.shape[:2]
    nb = t // QBLK
    qblocks = jnp.moveaxis(qb.reshape(bn, nb, QBLK, NG, HB, DH), 1, 0)

    def one_block(args):
        i, qblk = args
        q_idx = i * QBLK + jnp.arange(QBLK)
        outs, lses = [], []
        for gi, (win, dil) in enumerate(PATTERNS):
            o, l = dilated_attn(qblk[:, :, gi], kb[:, :, gi], vb[:, :, gi], q_idx, dil, win)
            outs.append(o)
            lses.append(l)
        return mix_groups(outs, lses)

    ob = lax.map(one_block, (jnp.arange(nb), qblocks))
    return jnp.moveaxis(ob, 0, 1).reshape(bn, t, HB, DH)


def dilated_sample(qb, kb, vb, bufs):
    t = qb.shape[1]
    outs, lses, new_bufs = [], [], []
    for gi, (win, dil) in enumerate(PATTERNS):
        buf = bufs[gi]
        n_buf = buf.shape[1]
        kv_new = jnp.stack([kb[:, :, gi], vb[:, :, gi]], axis=2)
        kv_all = jnp.concatenate([buf.astype(kv_new.dtype), kv_new], axis=1)
        q_idx = n_buf + jnp.arange(t)
        o, l = dilated_attn(qb[:, :, gi], kv_all[:, :, 0], kv_all[:, :, 1], q_idx, dil, win)
        outs.append(o)
        lses.append(l)
        new_bufs.append(kv_all[:, t:])
    return mix_groups(outs, lses), tuple(new_bufs)


def hybrid_layer(x, conv_buf, s0, kv_bufs, is_prompt, ln_in, w_in, conv_w, a_log, dt_bias, norm_a,
                 w_proj_a, w_proj_b, w_out):
    bn, t = x.shape[:2]
    xn = rmsnorm(x, ln_in)
    proj = jnp.einsum('btd,de->bte', xn, w_in)
    (q_a, k_a, v_a, z_a, b_a, a_a, q_b, k_b, v_b, z_b, gate_a, gate_b) = jnp.split(proj, SPLIT_IDX, axis=-1)

    mixed, conv_new = causal_conv_silu(jnp.concatenate([q_a, k_a, v_a], axis=-1), conv_buf, conv_w)
    q_a, k_a, v_a = jnp.split(mixed, (QK_DIM, 2 * QK_DIM), axis=-1)
    q_a = jnp.repeat(l2norm(q_a.reshape(bn, t, HK, DK)), HV // HK, axis=2) * (DK ** -0.5)
    k_a = jnp.repeat(l2norm(k_a.reshape(bn, t, HK, DK)), HV // HK, axis=2)
    v_a = v_a.reshape(bn, t, HV, DV).astype(jnp.float32)
    beta = jax.nn.sigmoid(b_a.astype(jnp.float32))
    g = -jnp.exp(a_log.astype(jnp.float32)) * jax.nn.softplus(a_a.astype(jnp.float32) + dt_bias.astype(jnp.float32))
    s0 = s0.astype(jnp.float32)
    if is_prompt:
        o_a, s_new = gated_delta_chunked(q_a, k_a, v_a, beta, g, s0)
    else:
        o_a, s_new = gated_delta_recurrent(q_a, k_a, v_a, beta, g, s0)
    o_a = gated_rmsnorm(o_a, z_a.reshape(bn, t, HV, DV), norm_a).reshape(bn, t, V_DIM)
    y_a = jnp.einsum('bte,ed->btd', o_a, w_proj_a)

    q_b = q_b.reshape(bn, t, NG, HB, DH)
    k_b = k_b.reshape(bn, t, NG, HB, DH)
    v_b = v_b.reshape(bn, t, NG, HB, DH)
    if is_prompt:
        o_b = dilated_prompt(q_b, k_b, v_b)
        new_bufs = tuple(jnp.stack([k_b[:, :, gi], v_b[:, :, gi]], axis=2)[:, t - min(win, t):]
                         for gi, (win, _) in enumerate(PATTERNS))
    else:
        o_b, new_bufs = dilated_sample(q_b, k_b, v_b, kv_bufs)
    o_b = o_b.reshape(bn, t, B_OUT) * jax.nn.silu(z_b)
    y_b = jnp.einsum('bte,ed->btd', o_b, w_proj_b)

    merged = jax.nn.sigmoid(gate_a) * y_a + jax.nn.sigmoid(gate_b) * y_b
    h = x + jnp.einsum('btd,de->bte', merged, w_out)
    return h, new_bufs, s_new, conv_new


def setup_inputs(seed: int = 0) -> dict:
    key = jax.random.key(seed)
    ks = jax.random.split(key, 20)

    def nrm(k, shape, scale):
        return jax.random.normal(k, shape, jnp.float32) * scale

    x_prompt = nrm(ks[0], (BATCH, SEQ, D_MODEL), 1.0)
    x_sample = nrm(ks[1], (DEC_BATCH, DEC_SEQ, D_MODEL), 1.0)
    cache_kv_w128 = nrm(ks[2], (DEPTH, DEC_BATCH, min(PATTERNS[0][0], PAST_LEN), 2, HB, DH), 1.0)
    cache_kv_w512 = nrm(ks[3], (DEPTH, DEC_BATCH, min(PATTERNS[1][0], PAST_LEN), 2, HB, DH), 1.0)
    cache_kv_w2048 = nrm(ks[4], (DEPTH, DEC_BATCH, min(PATTERNS[2][0], PAST_LEN), 2, HB, DH), 1.0)
    state_delta = nrm(ks[5], (DEPTH, DEC_BATCH, HV, DK, DV), 0.1)
    state_conv = nrm(ks[6], (DEPTH, DEC_BATCH, CONV_K - 1, CONV_DIM), 1.0)
    ln_in = 1.0 + nrm(ks[7], (DEPTH, D_MODEL), 0.02)
    w_in = nrm(ks[8], (DEPTH, D_MODEL, IN_DIM), D_MODEL ** -0.5)
    conv_w = nrm(ks[9], (DEPTH, CONV_K, CONV_DIM), CONV_K ** -0.5)
    a_log = jnp.log(jax.random.uniform(ks[10], (DEPTH, HV), jnp.float32, minval=1.0, maxval=16.0))
    dt = jnp.exp(jax.random.uniform(ks[11], (DEPTH, HV), jnp.float32, minval=math.log(1e-3), maxval=math.log(1e-1)))
    dt_bias = dt + jnp.log(-jnp.expm1(-dt))
    norm_a = 1.0 + nrm(ks[12], (DEPTH, DV), 0.02)
    w_proj_a = nrm(ks[13], (DEPTH, V_DIM, D_MODEL), V_DIM ** -0.5)
    w_proj_b = nrm(ks[14], (DEPTH, B_OUT, D_MODEL), B_OUT ** -0.5)
    w_out = nrm(ks[15], (DEPTH, D_MODEL, D_MODEL), D_MODEL ** -0.5)
    ln_f = 1.0 + nrm(ks[16], (D_MODEL,), 0.02)
    return {'x_prompt': x_prompt, 'x_sample': x_sample,
            'cache_kv_w128': cache_kv_w128, 'cache_kv_w512': cache_kv_w512, 'cache_kv_w2048': cache_kv_w2048,
            'state_delta': state_delta, 'state_conv': state_conv,
            'ln_in': ln_in, 'w_in': w_in, 'conv_w': conv_w, 'a_log': a_log, 'dt_bias': dt_bias,
            'norm_a': norm_a, 'w_proj_a': w_proj_a, 'w_proj_b': w_proj_b, 'w_out': w_out, 'ln_f': ln_f}


def reference(x_prompt, x_sample, cache_kv_w128, cache_kv_w512, cache_kv_w2048, state_delta, state_conv,
              ln_in, w_in, conv_w, a_log, dt_bias, norm_a, w_proj_a, w_proj_b, w_out, ln_f):
    h_p, h_s = x_prompt, x_sample
    p_kv128, p_kv512, p_kv2048, p_delta, p_conv = [], [], [], [], []
    s_kv128, s_kv512, s_kv2048, s_delta, s_conv = [], [], [], [], []
    for layer in range(DEPTH):
        weights = (ln_in[layer], w_in[layer], conv_w[layer], a_log[layer], dt_bias[layer], norm_a[layer],
                   w_proj_a[layer], w_proj_b[layer], w_out[layer])
        bp = h_p.shape[0]
        conv0 = jnp.zeros((bp, CONV_K - 1, CONV_DIM), h_p.dtype)
        s0 = jnp.zeros((bp, HV, DK, DV), jnp.float32)
        h_p, kv_p, d_p, c_p = hybrid_layer(h_p, conv0, s0, None, True, *weights)
        bufs = (cache_kv_w128[layer], cache_kv_w512[layer], cache_kv_w2048[layer])
        h_s, kv_s, d_s, c_s = hybrid_layer(h_s, state_conv[layer], state_delta[layer], bufs, False, *weights)
        p_kv128.append(kv_p[0]); p_kv512.append(kv_p[1]); p_kv2048.append(kv_p[2])
        p_delta.append(d_p); p_conv.append(c_p)
        s_kv128.append(kv_s[0]); s_kv512.append(kv_s[1]); s_kv2048.append(kv_s[2])
        s_delta.append(d_s); s_conv.append(c_s)
    y_prompt = rmsnorm(h_p, ln_f)
    y_sample = rmsnorm(h_s, ln_f)
    return (y_prompt, y_sample,
            jnp.stack(p_kv128), jnp.stack(p_kv512), jnp.stack(p_kv2048), jnp.stack(p_delta), jnp.stack(p_conv),
            jnp.stack(s_kv128), jnp.stack(s_kv512), jnp.stack(s_kv2048), jnp.stack(s_delta), jnp.stack(s_conv))
```

```python
import functools

import jax
import jax.numpy as jnp
import numpy as np
from jax import lax
from jax.experimental import pallas as pl
from jax.experimental.pallas import tpu as pltpu

F32 = jnp.float32
BF16 = jnp.bfloat16

D_MODEL = 2048
HK, HV, DK, DV = 16, 32, 128, 128
CONV_K = 4
CHUNK = 64
HB, DH = 8, 128
PATTERNS = ((128, 1), (512, 4), (2048, 16))
NG = len(PATTERNS)
EPS = 1e-6
NEG_INF = -1e30

QK_DIM = HK * DK
V_DIM = HV * DV
CONV_DIM = 2 * QK_DIM + V_DIM
B_DIM = NG * HB * DH
B_OUT = HB * DH
SPLIT_SIZES = (QK_DIM, QK_DIM, V_DIM, V_DIM, HV, HV, B_DIM, B_DIM, B_DIM, B_OUT, D_MODEL, D_MODEL)
SPLIT_OFFS = tuple(int(s) for s in np.cumsum((0,) + SPLIT_SIZES))

C_QA, C_KA, C_VA = 0, QK_DIM, 2 * QK_DIM
C_ZA = CONV_DIM
C_QB = C_ZA + V_DIM
C_KB = C_QB + B_DIM
C_VB = C_KB + B_DIM
C_ZB = C_VB + B_DIM
C_GA = C_ZB + B_OUT
C_GB = C_GA + D_MODEL
N_MAIN = C_GB + D_MODEL

LANE = 128
GH = 8
NGRP = HV // GH
VMEM_LIMIT = 56 * 1024 * 1024


def _cparams(sem):
    return pltpu.CompilerParams(dimension_semantics=sem, vmem_limit_bytes=VMEM_LIMIT)


def _silu(x):
    return x * jax.nn.sigmoid(x)


def _softplus(x):
    return jnp.maximum(x, 0.0) + jnp.log1p(jnp.exp(-jnp.abs(x)))


def _rms_rows(x, w):
    r = lax.rsqrt(jnp.mean(x * x, axis=-1, keepdims=True) + EPS)
    return x * r * w


def _dot(a, b):
    return jnp.dot(a, b, preferred_element_type=F32)


def _dot_nt(a, b):
    return lax.dot_general(a, b, (((1,), (1,)), ((), ())), preferred_element_type=F32)


def _dot_tn(a, b):
    return lax.dot_general(a, b, (((0,), (0,)), ((), ())), preferred_element_type=F32)


def _inproj_kernel(x_ref, ln_ref, w_ref, o_ref, xn_ref):
    @pl.when(pl.program_id(1) == 0)
    def _():
        xn_ref[...] = _rms_rows(x_ref[...], ln_ref[...]).astype(BF16)

    o_ref[...] = _dot(xn_ref[...], w_ref[...])


def _inproj(x, ln, w, tm, tn):
    m, d = x.shape
    n = w.shape[1]
    return pl.pallas_call(
        _inproj_kernel,
        grid=(m // tm, n // tn),
        in_specs=[pl.BlockSpec((tm, d), lambda i, j: (i, 0)),
                  pl.BlockSpec((1, d), lambda i, j: (0, 0)),
                  pl.BlockSpec((d, tn), lambda i, j: (0, j))],
        out_specs=pl.BlockSpec((tm, tn), lambda i, j: (i, j)),
        out_shape=jax.ShapeDtypeStruct((m, n), F32),
        scratch_shapes=[pltpu.VMEM((tm, d), BF16)],
        compiler_params=_cparams(("parallel", "arbitrary")),
        name="inproj",
    )(x, ln, w)


def _beta_g(x_ref, ln_ref, w_ref, alog_ref, dtb_ref):
    xn = _rms_rows(x_ref[...], ln_ref[...]).astype(BF16)
    p = _dot(xn, w_ref[...])
    beta = jax.nn.sigmoid(p[:, :LANE])
    g = -jnp.exp(alog_ref[...]) * _softplus(p[:, LANE:] + dtb_ref[...])
    return beta, g


def _ba_chunked_kernel(x_ref, ln_ref, w_ref, alog_ref, dtb_ref, betac_ref, gc_ref, gr_ref, *, tm):
    beta, g = _beta_g(x_ref, ln_ref, w_ref, alog_ref, dtb_ref)
    row = lax.broadcasted_iota(jnp.int32, (tm, tm), 0)
    col = lax.broadcasted_iota(jnp.int32, (tm, tm), 1)
    same = lax.shift_right_logical(row, 6) == lax.shift_right_logical(col, 6)
    tri = jnp.where(jnp.logical_and(same, col <= row), 1.0, 0.0).astype(F32)
    gcum = jnp.dot(tri, g, preferred_element_type=F32, precision=lax.Precision.HIGHEST)
    gcum_t = gcum.T
    for a in range(NGRP):
        betac_ref[a] = beta[:, a * GH:(a + 1) * GH]
        gc_ref[a] = gcum[:, a * GH:(a + 1) * GH]
        for cc in range(tm // CHUNK):
            gr_ref[a, cc] = gcum_t[a * GH:(a + 1) * GH, cc * CHUNK:(cc + 1) * CHUNK]


def _ba_chunked(x, ln, w_ba, alog, dtb, tm):
    m, d = x.shape
    return pl.pallas_call(
        functools.partial(_ba_chunked_kernel, tm=tm),
        grid=(m // tm,),
        in_specs=[pl.BlockSpec((tm, d), lambda i: (i, 0)),
                  pl.BlockSpec((1, d), lambda i: (0, 0)),
                  pl.BlockSpec((d, 2 * LANE), lambda i: (0, 0)),
                  pl.BlockSpec((1, LANE), lambda i: (0, 0)),
                  pl.BlockSpec((1, LANE), lambda i: (0, 0))],
        out_specs=[pl.BlockSpec((NGRP, tm, GH), lambda i: (0, i, 0)),
                   pl.BlockSpec((NGRP, tm, GH), lambda i: (0, i, 0)),
                   pl.BlockSpec((NGRP, tm // CHUNK, GH, CHUNK), lambda i: (0, i, 0, 0))],
        out_shape=[jax.ShapeDtypeStruct((NGRP, m, GH), F32),
                   jax.ShapeDtypeStruct((NGRP, m, GH), F32),
                   jax.ShapeDtypeStruct((NGRP, m // CHUNK, GH, CHUNK), F32)],
        compiler_params=_cparams(("parallel",)),
        name="beta_decay_prompt",
    )(x, ln, w_ba, alog, dtb)


def _ba_plain_kernel(x_ref, ln_ref, w_ref, alog_ref, dtb_ref, beta_ref, g_ref):
    beta, g = _beta_g(x_ref, ln_ref, w_ref, alog_ref, dtb_ref)
    beta_ref[...] = beta
    g_ref[...] = g


def _ba_plain(x, ln, w_ba, alog, dtb):
    m, d = x.shape
    return pl.pallas_call(
        _ba_plain_kernel,
        out_shape=[jax.ShapeDtypeStruct((m, LANE), F32), jax.ShapeDtypeStruct((m, LANE), F32)],
        compiler_params=pltpu.CompilerParams(vmem_limit_bytes=VMEM_LIMIT),
        name="beta_decay_sample",
    )(x, ln, w_ba, alog, dtb)


def _gdn_prompt_kernel(q_ref, k_ref, v_ref, z_ref, betac_ref, gc_ref, gr_ref,
                       cwq_ref, cwk_ref, cwv_ref, norm_ref,
                       o_ref, sfin_ref,
                       s_ref, xq_ref, xk_ref, xv_ref, qn_ref, kn_ref, vn_ref, *, rows):
    c = pl.program_id(1)
    nq = GH // (HV // HK)
    tail = 8

    @pl.when(c == 0)
    def _():
        s_ref[...] = jnp.zeros_like(s_ref)
        xq_ref[0:tail] = jnp.zeros((tail, xq_ref.shape[1]), F32)
        xk_ref[0:tail] = jnp.zeros((tail, xk_ref.shape[1]), F32)
        xv_ref[0:tail] = jnp.zeros((tail, xv_ref.shape[1]), F32)

    def conv(x_ref, xp_ref, w_ref):
        xp_ref[tail:tail + rows] = x_ref[...]
        y = xp_ref[pl.ds(tail - (CONV_K - 1), rows), :] * w_ref[0:1, :]
        for i in range(1, CONV_K):
            y = y + xp_ref[pl.ds(tail - (CONV_K - 1) + i, rows), :] * w_ref[i:i + 1, :]
        xp_ref[0:tail] = x_ref[rows - tail:rows, :]
        return _silu(y)

    def l2n(x):
        return x * lax.rsqrt(jnp.sum(x * x, axis=-1, keepdims=True) + EPS)

    qc = conv(q_ref, xq_ref, cwq_ref)
    kc = conv(k_ref, xk_ref, cwk_ref)
    for a in range(nq):
        sl = slice(a * DK, (a + 1) * DK)
        qn_ref[:, sl] = l2n(qc[:, sl]) * (DK ** -0.5)
        kn_ref[:, sl] = l2n(kc[:, sl])
    vn_ref[...] = conv(v_ref, xv_ref, cwv_ref)

    ri = lax.broadcasted_iota(jnp.int32, (CHUNK, CHUNK), 0)
    ci = lax.broadcasted_iota(jnp.int32, (CHUNK, CHUNK), 1)
    causal = ci <= ri
    strict = ci < ri
    eye = jnp.where(ci == ri, 1.0, 0.0).astype(F32)
    norm_w = norm_ref[...]

    def chunk_body(j, carry):
        r0 = pl.multiple_of(j * CHUNK, CHUNK)
        rsl = pl.ds(r0, CHUNK)
        gcol_all = gc_ref[rsl, :]
        bcol_all = betac_ref[rsl, :]
        grow_all = gr_ref[j]
        for a in range(nq):
            sl = slice(a * DK, (a + 1) * DK)
            qh = qn_ref[rsl, sl]
            kh = kn_ref[rsl, sl]
            khb = kh.astype(BF16)
            kk = _dot_nt(khb, khb)
            qk = _dot_nt(qh.astype(BF16), khb)
            for e in range(HV // HK):
                i = a * (HV // HK) + e
                vsl = slice(i * DV, (i + 1) * DV)
                gcol = gcol_all[:, i:i + 1]
                grow = grow_all[i:i + 1, :]
                bcol = bcol_all[:, i:i + 1]
                decay = jnp.exp(jnp.where(causal, gcol - grow, NEG_INF))
                lmat = jnp.where(strict, kk * bcol * decay, 0.0)
                lb = lmat.astype(BF16)
                pw = _dot(lb, lb)
                tinv = eye - lmat
                tinv = tinv + _dot(tinv.astype(BF16), pw.astype(BF16))
                span = 4
                while span < CHUNK:
                    pb = pw.astype(BF16)
                    pw = _dot(pb, pb)
                    tinv = tinv + _dot(tinv.astype(BF16), pw.astype(BF16))
                    span *= 2
                eg = jnp.exp(gcol)
                glast = gcol[CHUNK - 1:CHUNK, :]
                tb = tinv.astype(BF16)
                w = _dot(tb, (kh * (bcol * eg)).astype(BF16))
                u = _dot(tb, (vn_ref[rsl, vsl] * bcol).astype(BF16))
                intra = jnp.where(causal, qk * decay, 0.0)
                qg = (qh * eg).astype(BF16)
                kd = (kh * jnp.exp(glast - gcol)).astype(BF16)
                s_old = s_ref[i]
                sb = s_old.astype(BF16)
                vnew = u - _dot(w.astype(BF16), sb)
                vnb = vnew.astype(BF16)
                o = _dot(qg, sb) + _dot(intra.astype(BF16), vnb)
                s_ref[i] = s_old * jnp.exp(glast) + _dot_tn(kd, vnb)
                on = o * lax.rsqrt(jnp.mean(o * o, axis=-1, keepdims=True) + EPS) * norm_w
                o_ref[rsl, vsl] = (on * _silu(z_ref[rsl, vsl])).astype(o_ref.dtype)
        return carry

    lax.fori_loop(0, rows // CHUNK, chunk_body, 0)

    @pl.when(c == pl.num_programs(1) - 1)
    def _():
        sfin_ref[...] = s_ref[...]


def _gdn_prompt(proj, betac, gc, gr, conv_w, norm_a, rows):
    t = proj.shape[0]
    nblk = t // rows
    nq = GH // (HV // HK)
    wq, wv = nq * DK, GH * DV
    kq0, kv0, kz0 = C_KA // wq, C_VA // wv, C_ZA // wv
    scr = [pltpu.VMEM((GH, DK, DV), F32),
           pltpu.VMEM((rows + 8, wq), F32), pltpu.VMEM((rows + 8, wq), F32), pltpu.VMEM((rows + 8, wv), F32),
           pltpu.VMEM((rows, wq), F32), pltpu.VMEM((rows, wq), F32), pltpu.VMEM((rows, wv), F32)]
    return pl.pallas_call(
        functools.partial(_gdn_prompt_kernel, rows=rows),
        grid=(NGRP, nblk),
        in_specs=[pl.BlockSpec((rows, wq), lambda g, c: (c, g)),
                  pl.BlockSpec((rows, wq), lambda g, c: (c, kq0 + g)),
                  pl.BlockSpec((rows, wv), lambda g, c: (c, kv0 + g)),
                  pl.BlockSpec((rows, wv), lambda g, c: (c, kz0 + g)),
                  pl.BlockSpec((None, rows, GH), lambda g, c: (g, c, 0)),
                  pl.BlockSpec((None, rows, GH), lambda g, c: (g, c, 0)),
                  pl.BlockSpec((None, rows // CHUNK, GH, CHUNK), lambda g, c: (g, c, 0, 0)),
                  pl.BlockSpec((CONV_K, wq), lambda g, c: (0, g)),
                  pl.BlockSpec((CONV_K, wq), lambda g, c: (0, kq0 + g)),
                  pl.BlockSpec((CONV_K, wv), lambda g, c: (0, kv0 + g)),
                  pl.BlockSpec((1, DV), lambda g, c: (0, 0))],
        out_specs=[pl.BlockSpec((rows, wv), lambda g, c: (c, g)),
                   pl.BlockSpec((GH, DK, DV), lambda g, c: (g, 0, 0))],
        out_shape=[jax.ShapeDtypeStruct((t, V_DIM), BF16),
                   jax.ShapeDtypeStruct((HV, DK, DV), F32)],
        scratch_shapes=scr,
        compiler_params=_cparams(("parallel", "arbitrary")),
        name="gdn_prompt",
    )(proj, proj, proj, proj, betac, gc, gr, conv_w, conv_w, conv_w, norm_a)


def _gdn_sample_kernel(q_ref, k_ref, v_ref, z_ref, beta_ref, g_ref, cq_ref, ck_ref, cv_ref,
                       cwq_ref, cwk_ref, cwv_ref, norm_ref, s0_ref, o_ref, sout_ref, *, nt):
    nq = GH // (HV // HK)

    def conv_rows(x_ref, st_ref, w_ref):
        rows = [st_ref[i:i + 1, :] for i in range(CONV_K - 1)] + [x_ref[t:t + 1, :] for t in range(nt)]
        out = []
        for t in range(nt):
            y = rows[t] * w_ref[0:1, :]
            for i in range(1, CONV_K):
                y = y + rows[t + i] * w_ref[i:i + 1, :]
            out.append(_silu(y))
        return out

    def l2n(x):
        return x * lax.rsqrt(jnp.sum(x * x, axis=-1, keepdims=True) + EPS)

    qrows = conv_rows(q_ref, cq_ref, cwq_ref)
    krows = conv_rows(k_ref, ck_ref, cwk_ref)
    vrows = conv_rows(v_ref, cv_ref, cwv_ref)
    norm_w = norm_ref[...]
    zpad = jnp.zeros((LANE - 2 * nt, DK), F32)
    for a in range(nq):
        sl = slice(a * DK, (a + 1) * DK)
        kq = [l2n(krows[t][:, sl]) for t in range(nt)] + [l2n(qrows[t][:, sl]) * (DK ** -0.5) for t in range(nt)]
        cols = jnp.concatenate(kq + [zpad], axis=0).T
        for e in range(HV // HK):
            i = a * (HV // HK) + e
            vsl = slice(i * DV, (i + 1) * DV)
            s = s0_ref[i]
            for t in range(nt):
                kcol = cols[:, t:t + 1]
                qcol = cols[:, nt + t:nt + t + 1]
                s = s * jnp.exp(g_ref[t:t + 1, i:i + 1])
                ks = jnp.sum(s * kcol, axis=0, keepdims=True)
                vn = beta_ref[t:t + 1, i:i + 1] * (vrows[t][:, vsl] - ks)
                s = s + kcol * vn
                o = jnp.sum(s * qcol, axis=0, keepdims=True)
                on = o * lax.rsqrt(jnp.mean(o * o, axis=-1, keepdims=True) + EPS) * norm_w
                o_ref[t:t + 1, vsl] = on * _silu(z_ref[t:t + 1, vsl])
            sout_ref[i] = s


def _gdn_sample(proj3, beta4, g4, conv_state, conv_w, norm_a, s0):
    nb, nt, _ = proj3.shape
    nq = GH // (HV // HK)
    wq, wv = nq * DK, GH * DV
    kq0, kv0, kz0 = C_KA // wq, C_VA // wv, C_ZA // wv
    return pl.pallas_call(
        functools.partial(_gdn_sample_kernel, nt=nt),
        grid=(nb, NGRP),
        in_specs=[pl.BlockSpec((None, nt, wq), lambda b, g: (b, 0, g)),
                  pl.BlockSpec((None, nt, wq), lambda b, g: (b, 0, kq0 + g)),
                  pl.BlockSpec((None, nt, wv), lambda b, g: (b, 0, kv0 + g)),
                  pl.BlockSpec((None, nt, wv), lambda b, g: (b, 0, kz0 + g)),
                  pl.BlockSpec((None, None, nt, GH), lambda b, g: (g, b, 0, 0)),
                  pl.BlockSpec((None, None, nt, GH), lambda b, g: (g, b, 0, 0)),
                  pl.BlockSpec((None, CONV_K - 1, wq), lambda b, g: (b, 0, g)),
                  pl.BlockSpec((None, CONV_K - 1, wq), lambda b, g: (b, 0, kq0 + g)),
                  pl.BlockSpec((None, CONV_K - 1, wv), lambda b, g: (b, 0, kv0 + g)),
                  pl.BlockSpec((CONV_K, wq), lambda b, g: (0, g)),
                  pl.BlockSpec((CONV_K, wq), lambda b, g: (0, kq0 + g)),
                  pl.BlockSpec((CONV_K, wv), lambda b, g: (0, kv0 + g)),
                  pl.BlockSpec((1, DV), lambda b, g: (0, 0)),
                  pl.BlockSpec((None, GH, DK, DV), lambda b, g: (b, g, 0, 0))],
        out_specs=[pl.BlockSpec((None, nt, wv), lambda b, g: (b, 0, g)),
                   pl.BlockSpec((None, GH, DK, DV), lambda b, g: (b, g, 0, 0))],
        out_shape=[jax.ShapeDtypeStruct((nb, nt, V_DIM), F32),
                   jax.ShapeDtypeStruct((nb, HV, DK, DV), F32)],
        compiler_params=_cparams(("parallel", "arbitrary")),
        name="gdn_sample",
    )(proj3, proj3, proj3, proj3, beta4, g4, conv_state, conv_state, conv_state,
      conv_w, conv_w, conv_w, norm_a, s0)


def _attn_prompt_kernel(q_ref, kp_ref, kc_ref, vp_ref, vc_ref, o_ref, lse_ref, *, qblk):
    i = pl.program_id(1)
    row = lax.broadcasted_iota(jnp.int32, (qblk, qblk), 0)
    col = lax.broadcasted_iota(jnp.int32, (qblk, qblk), 1)
    mask_prev = jnp.logical_and(col >= row, i > 0)
    mask_cur = col <= row
    scale = DH ** -0.5
    lse_ref[...] = jnp.zeros_like(lse_ref)
    for h in range(HB):
        sl = slice(h * DH, (h + 1) * DH)
        q = q_ref[:, sl].astype(BF16)
        sp = jnp.where(mask_prev, _dot_nt(q, kp_ref[:, sl].astype(BF16)) * scale, NEG_INF)
        sc = jnp.where(mask_cur, _dot_nt(q, kc_ref[:, sl].astype(BF16)) * scale, NEG_INF)
        m = jnp.maximum(jnp.max(sp, axis=-1, keepdims=True), jnp.max(sc, axis=-1, keepdims=True))
        pp = jnp.exp(sp - m)
        pc = jnp.exp(sc - m)
        l = jnp.sum(pp, axis=-1, keepdims=True) + jnp.sum(pc, axis=-1, keepdims=True)
        acc = _dot(pp.astype(BF16), vp_ref[:, sl].astype(BF16)) + _dot(pc.astype(BF16), vc_ref[:, sl].astype(BF16))
        o_ref[:, sl] = acc / l
        lse_ref[:, h:h + 1] = m + jnp.log(l)


def _attn_prompt(proj, gi, dil, qblk):
    t, n = proj.shape
    ts = t // dil
    nblk = ts // qblk
    view = proj.reshape(ts, dil * n)
    w = HB * DH
    nb = n // w
    cq, ck, cv = C_QB // w + gi, C_KB // w + gi, C_VB // w + gi
    o, lse = pl.pallas_call(
        functools.partial(_attn_prompt_kernel, qblk=qblk),
        grid=(dil, nblk),
        in_specs=[pl.BlockSpec((qblk, w), lambda r, i: (i, r * nb + cq)),
                  pl.BlockSpec((qblk, w), lambda r, i: (jnp.maximum(i - 1, 0), r * nb + ck)),
                  pl.BlockSpec((qblk, w), lambda r, i: (i, r * nb + ck)),
                  pl.BlockSpec((qblk, w), lambda r, i: (jnp.maximum(i - 1, 0), r * nb + cv)),
                  pl.BlockSpec((qblk, w), lambda r, i: (i, r * nb + cv))],
        out_specs=[pl.BlockSpec((qblk, w), lambda r, i: (i, r)),
                   pl.BlockSpec((qblk, LANE), lambda r, i: (i, r))],
        out_shape=[jax.ShapeDtypeStruct((ts, dil * w), F32),
                   jax.ShapeDtypeStruct((ts, dil * LANE), F32)],
        compiler_params=_cparams(("parallel", "arbitrary")),
        name=f"attn_prompt_d{dil}",
    )(view, view, view, view, view)
    return o.reshape(t, w), lse.reshape(t, LANE)


def _attn_sample_kernel(q_ref, kn_ref, vn_ref, b0_ref, b1_ref, b2_ref, o_ref, *, nt):
    nkeys = b0_ref.shape[0]
    width = nkeys * HB
    lane = lax.broadcasted_iota(jnp.int32, (HB, width), 1)
    sub = lax.broadcasted_iota(jnp.int32, (HB, width), 0)
    head_ok = jnp.bitwise_and(lane, HB - 1) == sub
    key_idx = lax.shift_right_logical(lane, 3)
    scale = DH ** -0.5
    for t in range(nt):
        outs, lses = [], []
        for gi in range(NG):
            q = q_ref[t, gi]
            if gi == 0:
                kmat = b0_ref[:, 0].reshape(width, DH)
                vmat = b0_ref[:, 1].reshape(width, DH)
                valid = jnp.logical_and(head_ok, key_idx >= t)
                new_rows = range(t + 1)
            else:
                bref = b1_ref if gi == 1 else b2_ref
                kmat = bref[:, t, 0].reshape(width, DH)
                vmat = bref[:, t, 1].reshape(width, DH)
                valid = head_ok
                new_rows = (t,)
            s = jnp.where(valid, _dot_nt(q.astype(BF16), kmat.astype(BF16)) * scale, NEG_INF)
            s_new = [jnp.sum(q * kn_ref[c, gi], axis=-1, keepdims=True) * scale for c in new_rows]
            m = jnp.max(s, axis=-1, keepdims=True)
            for sn in s_new:
                m = jnp.maximum(m, sn)
            p = jnp.exp(s - m)
            l = jnp.sum(p, axis=-1, keepdims=True)
            acc = _dot(p.astype(BF16), vmat.astype(BF16))
            for c, sn in zip(new_rows, s_new):
                pn = jnp.exp(sn - m)
                l = l + pn
                acc = acc + pn * vn_ref[c, gi]
            outs.append(acc / l)
            lses.append(m + jnp.log(l))
        mx = jnp.maximum(jnp.maximum(lses[0], lses[1]), lses[2])
        es = [jnp.exp(x - mx) for x in lses]
        den = es[0] + es[1] + es[2]
        o_ref[t] = (es[0] * outs[0] + es[1] * outs[1] + es[2] * outs[2]) / den


def _attn_sample(q5, k5, v5, buf128, buf512, buf2048):
    nb, nt = q5.shape[:2]
    new_spec = pl.BlockSpec((None, nt, NG, HB, DH), lambda b: (b, 0, 0, 0, 0))
    w0 = buf128.shape[1]
    v1 = buf512.reshape(nb, buf512.shape[1] // 4, 4, 2, HB, DH)
    v2 = buf2048.reshape(nb, buf2048.shape[1] // 16, 16, 2, HB, DH)
    return pl.pallas_call(
        functools.partial(_attn_sample_kernel, nt=nt),
        grid=(nb,),
        in_specs=[new_spec, new_spec, new_spec,
                  pl.BlockSpec((None, w0, 2, HB, DH), lambda b: (b, 0, 0, 0, 0)),
                  pl.BlockSpec((None, v1.shape[1], 4, 2, HB, DH), lambda b: (b, 0, 0, 0, 0, 0)),
                  pl.BlockSpec((None, v2.shape[1], nt, 2, HB, DH), lambda b: (b, 0, 0, 0, 0, 0))],
        out_specs=pl.BlockSpec((None, nt, HB, DH), lambda b: (b, 0, 0, 0)),
        out_shape=jax.ShapeDtypeStruct((nb, nt, HB, DH), F32),
        compiler_params=_cparams(("parallel",)),
        name="attn_sample",
    )(q5, k5, v5, buf128, v1, v2)


def _roll_kernel(b0_ref, b1_ref, b2_ref, n0_ref, n1_ref, n2_ref, o0_ref, o1_ref, o2_ref, *, nt):
    for b_ref, n_ref, o_ref in ((b0_ref, n0_ref, o0_ref), (b1_ref, n1_ref, o1_ref), (b2_ref, n2_ref, o2_ref)):
        w = b_ref.shape[0]

        def body(r, carry, b_ref=b_ref, o_ref=o_ref):
            o_ref[pl.ds(r * nt, nt)] = b_ref[pl.ds(r * nt + nt, nt)]
            return carry

        lax.fori_loop(0, w // nt - 1, body, 0)
        o_ref[w - nt:w] = n_ref[...]


def _roll_buffers(bufs, news):
    nb = bufs[0].shape[0]
    nt = news[0].shape[1]
    bspec = [pl.BlockSpec((None, b.shape[1], None, HB, DH), lambda b_, kv: (b_, 0, kv, 0, 0)) for b in bufs]
    nspec = [pl.BlockSpec((None, nt, None, HB, DH), lambda b_, kv: (b_, 0, kv, 0, 0)) for _ in news]
    return pl.pallas_call(
        functools.partial(_roll_kernel, nt=nt),
        grid=(nb, 2),
        in_specs=bspec + nspec,
        out_specs=bspec,
        out_shape=[jax.ShapeDtypeStruct(b.shape, b.dtype) for b in bufs],
        compiler_params=_cparams(("parallel", "arbitrary")),
        name="roll_kv",
    )(*bufs, *news)


def _mix_kernel(o0_ref, o1_ref, o2_ref, l0_ref, l1_ref, l2_ref, z_ref, ob_ref):
    l0, l1, l2 = l0_ref[...], l1_ref[...], l2_ref[...]
    mx = jnp.maximum(jnp.maximum(l0, l1), l2)
    e0, e1, e2 = jnp.exp(l0 - mx), jnp.exp(l1 - mx), jnp.exp(l2 - mx)
    inv = 1.0 / (e0 + e1 + e2)
    w0, w1, w2 = e0 * inv, e1 * inv, e2 * inv
    for h in range(HB):
        sl = slice(h * DH, (h + 1) * DH)
        hs = slice(h, h + 1)
        mixed = w0[:, hs] * o0_ref[:, sl] + w1[:, hs] * o1_ref[:, sl] + w2[:, hs] * o2_ref[:, sl]
        ob_ref[:, sl] = (mixed * _silu(z_ref[:, sl])).astype(ob_ref.dtype)


def _mix(outs, lses, proj, tm):
    t = proj.shape[0]
    w = HB * DH
    ospec = pl.BlockSpec((tm, w), lambda i: (i, 0))
    lspec = pl.BlockSpec((tm, LANE), lambda i: (i, 0))
    return pl.pallas_call(
        _mix_kernel,
        grid=(t // tm,),
        in_specs=[ospec, ospec, ospec, lspec, lspec, lspec,
                  pl.BlockSpec((tm, w), lambda i: (i, C_ZB // w))],
        out_specs=ospec,
        out_shape=jax.ShapeDtypeStruct((t, w), BF16),
        compiler_params=_cparams(("parallel",)),
        name="group_mix",
    )(*outs, *lses, proj)


def _gate_b_kernel(o_ref, z_ref, ob_ref):
    ob_ref[...] = (o_ref[...] * _silu(z_ref[...])).astype(ob_ref.dtype)


def _gate_b(o, proj):
    t, w = o.shape
    return pl.pallas_call(
        _gate_b_kernel,
        grid=(1,),
        in_specs=[pl.BlockSpec((t, w), lambda i: (0, 0)),
                  pl.BlockSpec((t, w), lambda i: (0, C_ZB // w))],
        out_specs=pl.BlockSpec((t, w), lambda i: (0, 0)),
        out_shape=jax.ShapeDtypeStruct((t, w), BF16),
        name="gate_b_sample",
    )(o, proj)


def _merge_kernel(oa_ref, ob_ref, wa_ref, wb_ref, ga_ref, gb_ref, m_ref):
    ya = _dot(oa_ref[...], wa_ref[...])
    yb = _dot(ob_ref[...], wb_ref[...])
    m_ref[...] = (jax.nn.sigmoid(ga_ref[...]) * ya + jax.nn.sigmoid(gb_ref[...]) * yb).astype(m_ref.dtype)


def _merge(oa, ob, wa, wb, proj, tm, tn):
    t = oa.shape[0]
    return pl.pallas_call(
        _merge_kernel,
        grid=(t // tm, D_MODEL // tn),
        in_specs=[pl.BlockSpec((tm, V_DIM), lambda i, j: (i, 0)),
                  pl.BlockSpec((tm, B_OUT), lambda i, j: (i, 0)),
                  pl.BlockSpec((V_DIM, tn), lambda i, j: (0, j)),
                  pl.BlockSpec((B_OUT, tn), lambda i, j: (0, j)),
                  pl.BlockSpec((tm, tn), lambda i, j: (i, C_GA // tn + j)),
                  pl.BlockSpec((tm, tn), lambda i, j: (i, C_GB // tn + j))],
        out_specs=pl.BlockSpec((tm, tn), lambda i, j: (i, j)),
        out_shape=jax.ShapeDtypeStruct((t, D_MODEL), BF16),
        compiler_params=_cparams(("parallel", "arbitrary")),
        name="merge",
    )(oa, ob, wa, wb, proj, proj)


def _out_kernel(m_ref, x_ref, w_ref, ln_ref, y_ref):
    h = x_ref[...] + _dot(m_ref[...], w_ref[...])
    y_ref[...] = _rms_rows(h, ln_ref[...])


def _out(merged, x, w_out, ln_f, tm):
    t = x.shape[0]
    return pl.pallas_call(
        _out_kernel,
        grid=(t // tm,),
        in_specs=[pl.BlockSpec((tm, D_MODEL), lambda i: (i, 0)),
                  pl.BlockSpec((tm, D_MODEL), lambda i: (i, 0)),
                  pl.BlockSpec((D_MODEL, D_MODEL), lambda i: (0, 0)),
                  pl.BlockSpec((1, D_MODEL), lambda i: (0, 0))],
        out_specs=pl.BlockSpec((tm, D_MODEL), lambda i: (i, 0)),
        out_shape=jax.ShapeDtypeStruct((t, D_MODEL), F32),
        compiler_params=_cparams(("parallel",)),
        name="out_proj",
    )(merged, x, w_out, ln_f)


def _prep_weights(w_in, a_log, dt_bias):
    parts = [w_in[:, SPLIT_OFFS[i]:SPLIT_OFFS[i + 1]] for i in range(len(SPLIT_SIZES))]
    q_a, k_a, v_a, z_a, b_a, a_a, q_b, k_b, v_b, z_b, g_a, g_b = parts
    w_main = jnp.concatenate([q_a, k_a, v_a, z_a, q_b, k_b, v_b, z_b, g_a, g_b], axis=1).astype(BF16)
    pad = jnp.zeros((w_in.shape[0], LANE - HV), w_in.dtype)
    w_ba = jnp.concatenate([b_a, pad, a_a, pad], axis=1).astype(BF16)
    alog = jnp.pad(a_log, (0, LANE - HV)).reshape(1, LANE)
    dtb = jnp.pad(dt_bias, (0, LANE - HV)).reshape(1, LANE)
    return w_main, w_ba, alog, dtb


def _pick(n, pref):
    for c in pref:
        if n % c == 0:
            return c
    return n


def _tail(merged_in, x, w_out, ln_f):
    tm = _pick(x.shape[0], (512, 256, 128))
    return _out(merged_in, x, w_out, ln_f, tm)


def _prompt(x, ln_in, w_main, w_ba, alog, dtb, conv_w, norm_a, wa, wb, w_out, ln_f):
    t = x.shape[0]
    proj = _inproj(x, ln_in, w_main, _pick(t, (1024, 512, 256)), 1024)
    betac, gc, gr = _ba_chunked(x, ln_in, w_ba, alog, dtb, _pick(t, (256,)))
    oa, s_fin = _gdn_prompt(proj, betac, gc, gr, conv_w, norm_a, _pick(t, (256,)))
    outs, lses = [], []
    for gi, (_, dil) in enumerate(PATTERNS):
        o, l = _attn_prompt(proj, gi, dil, 128)
        outs.append(o)
        lses.append(l)
    ob = _mix(outs, lses, proj, _pick(t, (512, 256)))
    merged = _merge(oa, ob, wa, wb, proj, _pick(t, (512, 256)), 512)
    y = _tail(merged, x, w_out, ln_f)
    kvs = []
    for gi, (win, _) in enumerate(PATTERNS):
        n = min(win, t)
        kk = proj[t - n:, C_KB + gi * B_OUT:C_KB + (gi + 1) * B_OUT].reshape(n, HB, DH)
        vv = proj[t - n:, C_VB + gi * B_OUT:C_VB + (gi + 1) * B_OUT].reshape(n, HB, DH)
        kvs.append(jnp.stack([kk, vv], axis=1))
    conv_new = proj[t - (CONV_K - 1):, :CONV_DIM]
    return y, kvs, s_fin, conv_new


def _sample(xs, bufs, s0, conv0, ln_in, w_main, w_ba, alog, dtb, conv_w, norm_a, wa, wb, w_out, ln_f):
    nb, nt, d = xs.shape
    x = xs.reshape(nb * nt, d)
    proj = _inproj(x, ln_in, w_main, nb * nt, 1024)
    beta, g = _ba_plain(x, ln_in, w_ba, alog, dtb)
    beta4 = beta[:, :HV].reshape(nb, nt, NGRP, GH).transpose(2, 0, 1, 3)
    g4 = g[:, :HV].reshape(nb, nt, NGRP, GH).transpose(2, 0, 1, 3)
    proj3 = proj.reshape(nb, nt, N_MAIN)
    oa, s_new = _gdn_sample(proj3, beta4, g4, conv0, conv_w, norm_a, s0)
    oa = oa.reshape(nb * nt, V_DIM).astype(BF16)
    q5 = proj[:, C_QB:C_QB + B_DIM].reshape(nb, nt, NG, HB, DH)
    k5 = proj[:, C_KB:C_KB + B_DIM].reshape(nb, nt, NG, HB, DH)
    v5 = proj[:, C_VB:C_VB + B_DIM].reshape(nb, nt, NG, HB, DH)
    o_b = _attn_sample(q5, k5, v5, *bufs).reshape(nb * nt, B_OUT)
    ob = _gate_b(o_b, proj)
    merged = _merge(oa, ob, wa, wb, proj, nb * nt, 512)
    y = _tail(merged, x, w_out, ln_f).reshape(nb, nt, d)
    kv5 = jnp.stack([k5, v5], axis=2)
    news = [kv5[:, :, :, gi] for gi in range(NG)]
    new_bufs = _roll_buffers(bufs, news)
    conv_new = proj3[:, nt - (CONV_K - 1):, :CONV_DIM]
    return y, new_bufs, s_new, conv_new


def kernel(x_prompt, x_sample, cache_kv_w128, cache_kv_w512, cache_kv_w2048, state_delta, state_conv,
           ln_in, w_in, conv_w, a_log, dt_bias, norm_a, w_proj_a, w_proj_b, w_out, ln_f):
    w_main, w_ba, alog, dtb = _prep_weights(w_in[0], a_log[0], dt_bias[0])
    shared = (ln_in, w_main, w_ba, alog, dtb, conv_w[0], norm_a,
              w_proj_a[0].astype(BF16), w_proj_b[0].astype(BF16), w_out[0].astype(BF16), ln_f.reshape(1, -1))
    y_p, kv_p, d_p, c_p = _prompt(x_prompt[0], *shared)
    y_s, kv_s, d_s, c_s = _sample(x_sample, (cache_kv_w128[0], cache_kv_w512[0], cache_kv_w2048[0]),
                                  state_delta[0], state_conv[0], *shared)
    return (y_p[None], y_s,
            kv_p[0][None, None], kv_p[1][None, None], kv_p[2][None, None], d_p[None, None], c_p[None, None],
            kv_s[0][None], kv_s[1][None], kv_s[2][None], d_s[None], c_s[None])
```

```python
import functools

import jax
import jax.numpy as jnp
import numpy as np
from jax import lax
from jax.experimental import pallas as pl
from jax.experimental.pallas import tpu as pltpu

F32 = jnp.float32
BF16 = jnp.bfloat16

D_MODEL = 2048
HK, HV, DK, DV = 16, 32, 128, 128
CONV_K = 4
CHUNK = 64
HB, DH = 8, 128
PATTERNS = ((128, 1), (512, 4), (2048, 16))
NG = len(PATTERNS)
EPS = 1e-6
NEG_INF = -1e30

QK_DIM = HK * DK
V_DIM = HV * DV
CONV_DIM = 2 * QK_DIM + V_DIM
B_DIM = NG * HB * DH
B_OUT = HB * DH
SPLIT_SIZES = (QK_DIM, QK_DIM, V_DIM, V_DIM, HV, HV, B_DIM, B_DIM, B_DIM, B_OUT, D_MODEL, D_MODEL)
SPLIT_OFFS = tuple(int(s) for s in np.cumsum((0,) + SPLIT_SIZES))

C_QA, C_KA, C_VA = 0, QK_DIM, 2 * QK_DIM
C_ZA = CONV_DIM
C_QB = C_ZA + V_DIM
C_KB = C_QB + B_DIM
C_VB = C_KB + B_DIM
C_ZB = C_VB + B_DIM
C_GA = C_ZB + B_OUT
C_GB = C_GA + D_MODEL
N_MAIN = C_GB + D_MODEL

LANE = 128
GH = 8
PERM_ROWS = 256
NGRP = HV // GH
VMEM_LIMIT = 56 * 1024 * 1024


def _cparams(sem):
    return pltpu.CompilerParams(dimension_semantics=sem, vmem_limit_bytes=VMEM_LIMIT)


def _silu(x):
    return x * jax.nn.sigmoid(x)


def _softplus(x):
    return jnp.maximum(x, 0.0) + jnp.log1p(jnp.exp(-jnp.abs(x)))


def _rms_rows(x, w):
    r = lax.rsqrt(jnp.mean(x * x, axis=-1, keepdims=True) + EPS)
    return x * r * w


def _dot(a, b):
    return jnp.dot(a, b, preferred_element_type=F32)


def _dot_nt(a, b):
    return lax.dot_general(a, b, (((1,), (1,)), ((), ())), preferred_element_type=F32)


def _dot_tn(a, b):
    return lax.dot_general(a, b, (((0,), (0,)), ((), ())), preferred_element_type=F32)


def _bdot(a, b):
    return lax.dot_general(a, b, (((2,), (1,)), ((0,), (0,))), preferred_element_type=F32)


def _bdot_nt(a, b):
    return lax.dot_general(a, b, (((2,), (2,)), ((0,), (0,))), preferred_element_type=F32)


def _bdot_tn(a, b):
    return lax.dot_general(a, b, (((1,), (1,)), ((0,), (0,))), preferred_element_type=F32)


def _inproj_kernel(x_ref, ln_ref, w_ref, o_ref, xn_ref):
    @pl.when(pl.program_id(1) == 0)
    def _():
        xn_ref[...] = _rms_rows(x_ref[...], ln_ref[...]).astype(BF16)

    o_ref[...] = _dot(xn_ref[...], w_ref[...])


def _inproj(x, ln, w, col0, ncols, tm, tn):
    m, d = x.shape
    c0 = col0 // tn
    return pl.pallas_call(
        _inproj_kernel,
        grid=(m // tm, ncols // tn),
        in_specs=[pl.BlockSpec((tm, d), lambda i, j: (i, 0)),
                  pl.BlockSpec((1, d), lambda i, j: (0, 0)),
                  pl.BlockSpec((d, tn), lambda i, j: (0, c0 + j))],
        out_specs=pl.BlockSpec((tm, tn), lambda i, j: (i, j)),
        out_shape=jax.ShapeDtypeStruct((m, ncols), F32),
        scratch_shapes=[pltpu.VMEM((tm, d), BF16)],
        compiler_params=_cparams(("parallel", "arbitrary")),
        name="inproj",
    )(x, ln, w)


def _residue_index(i, n, dil):
    sub = n // dil
    return jnp.bitwise_and(i, sub - 1) * dil + lax.shift_right_logical(i, sub.bit_length() - 1)


def _perm_to_residue(n, dil):
    i = lax.broadcasted_iota(jnp.int32, (n, n), 0)
    j = lax.broadcasted_iota(jnp.int32, (n, n), 1)
    return j == _residue_index(i, n, dil)


def _inproj_res_kernel(x_ref, ln_ref, w_ref, o_ref, xn_ref, *, dil, tm):
    pg = min(tm, PERM_ROWS)
    sub = pg // dil

    @pl.when(pl.program_id(1) == 0)
    def _():
        xn = _rms_rows(x_ref[...], ln_ref[...]).astype(BF16)
        if dil == 1:
            xn_ref[...] = xn
        else:
            perm = jnp.where(_perm_to_residue(pg, dil), 1.0, 0.0).astype(BF16)
            for g in range(tm // pg):
                xn_ref[g * pg:(g + 1) * pg, :] = _dot(perm, xn[g * pg:(g + 1) * pg]).astype(BF16)

    acc = _dot(xn_ref[...], w_ref[...])
    for g in range(tm // pg):
        for r in range(dil):
            o_ref[r, g * sub:(g + 1) * sub, :] = acc[g * pg + r * sub:g * pg + (r + 1) * sub].astype(o_ref.dtype)


def _inproj_res(x, ln, w, gi, dil, tm):
    t, d = x.shape
    wd = HB * DH
    c0 = C_QB // wd + gi
    return pl.pallas_call(
        functools.partial(_inproj_res_kernel, dil=dil, tm=tm),
        grid=(t // tm, 3),
        in_specs=[pl.BlockSpec((tm, d), lambda i, j: (i, 0)),
                  pl.BlockSpec((1, d), lambda i, j: (0, 0)),
                  pl.BlockSpec((d, wd), lambda i, j: (0, c0 + NG * j))],
        out_specs=pl.BlockSpec((None, dil, tm // dil, wd), lambda i, j: (j, 0, i, 0)),
        out_shape=jax.ShapeDtypeStruct((3, dil, t // dil, wd), BF16),
        scratch_shapes=[pltpu.VMEM((tm, d), BF16)],
        compiler_params=_cparams(("parallel", "arbitrary")),
        name=f"inproj_attn_d{dil}",
    )(x, ln, w)


def _beta_g(x_ref, ln_ref, w_ref, alog_ref, dtb_ref):
    xn = _rms_rows(x_ref[...], ln_ref[...]).astype(BF16)
    p = _dot(xn, w_ref[...])
    beta = jax.nn.sigmoid(p[:, :LANE])
    g = -jnp.exp(alog_ref[...]) * _softplus(p[:, LANE:] + dtb_ref[...])
    return beta, g


def _ba_chunked_kernel(x_ref, ln_ref, w_ref, alog_ref, dtb_ref, betac_ref, gc_ref, gr_ref, *, tm):
    beta, g = _beta_g(x_ref, ln_ref, w_ref, alog_ref, dtb_ref)
    row = lax.broadcasted_iota(jnp.int32, (tm, tm), 0)
    col = lax.broadcasted_iota(jnp.int32, (tm, tm), 1)
    same = lax.shift_right_logical(row, 6) == lax.shift_right_logical(col, 6)
    tri = jnp.where(jnp.logical_and(same, col <= row), 1.0, 0.0).astype(F32)
    gcum = jnp.dot(tri, g, preferred_element_type=F32, precision=lax.Precision.HIGHEST)
    gcum_t = gcum.T
    for a in range(NGRP):
        betac_ref[a] = beta[:, a * GH:(a + 1) * GH]
        gc_ref[a] = gcum[:, a * GH:(a + 1) * GH]
        for cc in range(tm // CHUNK):
            gr_ref[a, cc] = gcum_t[a * GH:(a + 1) * GH, cc * CHUNK:(cc + 1) * CHUNK]


def _ba_chunked(x, ln, w_ba, alog, dtb, tm):
    m, d = x.shape
    return pl.pallas_call(
        functools.partial(_ba_chunked_kernel, tm=tm),
        grid=(m // tm,),
        in_specs=[pl.BlockSpec((tm, d), lambda i: (i, 0)),
                  pl.BlockSpec((1, d), lambda i: (0, 0)),
                  pl.BlockSpec((d, 2 * LANE), lambda i: (0, 0)),
                  pl.BlockSpec((1, LANE), lambda i: (0, 0)),
                  pl.BlockSpec((1, LANE), lambda i: (0, 0))],
        out_specs=[pl.BlockSpec((NGRP, tm, GH), lambda i: (0, i, 0)),
                   pl.BlockSpec((NGRP, tm, GH), lambda i: (0, i, 0)),
                   pl.BlockSpec((NGRP, tm // CHUNK, GH, CHUNK), lambda i: (0, i, 0, 0))],
        out_shape=[jax.ShapeDtypeStruct((NGRP, m, GH), F32),
                   jax.ShapeDtypeStruct((NGRP, m, GH), F32),
                   jax.ShapeDtypeStruct((NGRP, m // CHUNK, GH, CHUNK), F32)],
        compiler_params=_cparams(("parallel",)),
        name="beta_decay_prompt",
    )(x, ln, w_ba, alog, dtb)


def _ba_plain_kernel(x_ref, ln_ref, w_ref, alog_ref, dtb_ref, beta_ref, g_ref):
    beta, g = _beta_g(x_ref, ln_ref, w_ref, alog_ref, dtb_ref)
    beta_ref[...] = beta
    g_ref[...] = g


def _ba_plain(x, ln, w_ba, alog, dtb):
    m, d = x.shape
    return pl.pallas_call(
        _ba_plain_kernel,
        out_shape=[jax.ShapeDtypeStruct((m, LANE), F32), jax.ShapeDtypeStruct((m, LANE), F32)],
        compiler_params=pltpu.CompilerParams(vmem_limit_bytes=VMEM_LIMIT),
        name="beta_decay_sample",
    )(x, ln, w_ba, alog, dtb)


def _gdn_prompt_kernel(q_ref, k_ref, v_ref, z_ref, betac_ref, gc_ref, gr_ref,
                       cwq_ref, cwk_ref, cwv_ref, norm_ref,
                       o_ref, sfin_ref,
                       s_ref, xq_ref, xk_ref, xv_ref, qn_ref, kn_ref, vn_ref, *, rows):
    c_idx = pl.program_id(1)
    nq = GH // (HV // HK)
    tail = 8

    @pl.when(c_idx == 0)
    def _():
        s_ref[...] = jnp.zeros_like(s_ref)
        xq_ref[0:tail] = jnp.zeros((tail, xq_ref.shape[1]), F32)
        xk_ref[0:tail] = jnp.zeros((tail, xk_ref.shape[1]), F32)
        xv_ref[0:tail] = jnp.zeros((tail, xv_ref.shape[1]), F32)

    def conv(x_ref, xp_ref, w_ref):
        xp_ref[tail:tail + rows] = x_ref[...]
        y = xp_ref[pl.ds(tail - (CONV_K - 1), rows), :] * w_ref[0:1, :]
        for i in range(1, CONV_K):
            y = y + xp_ref[pl.ds(tail - (CONV_K - 1) + i, rows), :] * w_ref[i:i + 1, :]
        xp_ref[0:tail] = x_ref[rows - tail:rows, :]
        return _silu(y)

    def l2n(x):
        return x * lax.rsqrt(jnp.sum(x * x, axis=-1, keepdims=True) + EPS)

    qc = conv(q_ref, xq_ref, cwq_ref)
    kc = conv(k_ref, xk_ref, cwk_ref)
    for a in range(nq):
        sl = slice(a * DK, (a + 1) * DK)
        qn_ref[:, sl] = l2n(qc[:, sl]) * (DK ** -0.5)
        kn_ref[:, sl] = l2n(kc[:, sl])
    vn_ref[...] = conv(v_ref, xv_ref, cwv_ref)

    nc = rows // CHUNK
    rep = HV // HK
    ri = lax.broadcasted_iota(jnp.int32, (CHUNK, CHUNK), 0)
    ci = lax.broadcasted_iota(jnp.int32, (CHUNK, CHUNK), 1)
    causal = ci <= ri
    strict = ci < ri
    eye = jnp.where(ci == ri, 1.0, 0.0).astype(F32)
    norm_w = norm_ref[...]

    def rs(c):
        return slice(c * CHUNK, (c + 1) * CHUNK)

    def ls(h, w):
        return slice(h * w, (h + 1) * w)

    def stack_ch(fn):
        return jnp.stack([fn(c, h) for c in range(nc) for h in range(GH)], axis=0)

    k16 = jnp.stack([kn_ref[rs(c), ls(a, DK)] for c in range(nc) for a in range(nq)], axis=0)
    q16 = jnp.stack([qn_ref[rs(c), ls(a, DK)] for c in range(nc) for a in range(nq)], axis=0)
    k16b = k16.astype(BF16)
    kq16 = _bdot_nt(jnp.concatenate([k16b, q16.astype(BF16)], axis=1), k16b)
    kk = stack_ch(lambda c, h: kq16[c * nq + h // rep, :CHUNK, :])
    qk = stack_ch(lambda c, h: kq16[c * nq + h // rep, CHUNK:, :])
    k3 = stack_ch(lambda c, h: k16[c * nq + h // rep])
    q3 = stack_ch(lambda c, h: q16[c * nq + h // rep])
    v3 = stack_ch(lambda c, h: vn_ref[rs(c), ls(h, DV)])
    gcol = stack_ch(lambda c, h: gc_ref[rs(c), h:h + 1])
    bcol = stack_ch(lambda c, h: betac_ref[rs(c), h:h + 1])
    grow = stack_ch(lambda c, h: gr_ref[c, h:h + 1, :])
    glast = gcol[:, CHUNK - 1:CHUNK, :]
    eg = jnp.exp(gcol)
    decay = jnp.exp(jnp.where(causal, gcol - grow, NEG_INF))
    lmat = jnp.where(strict, kk * bcol * decay, 0.0)
    intra = jnp.where(causal, qk * decay, 0.0)
    lb = lmat.astype(BF16)
    pw = _bdot(lb, lb)
    tinv = eye - lmat
    span = 4
    while span < CHUNK:
        pb = pw.astype(BF16)
        both = _bdot(jnp.concatenate([tinv.astype(BF16), pb], axis=1), pb)
        tinv = tinv + both[:, :CHUNK, :]
        pw = both[:, CHUNK:, :]
        span *= 2
    tinv = tinv + _bdot(tinv.astype(BF16), pw.astype(BF16))
    rhs = jnp.concatenate([k3 * (bcol * eg), v3 * bcol], axis=2)
    wu = _bdot(tinv.astype(BF16), rhs.astype(BF16))
    iwu = _bdot(intra.astype(BF16), wu.astype(BF16))
    wq = jnp.concatenate([wu[:, :, :DK], q3 * eg - iwu[:, :, :DK]], axis=1).astype(BF16)
    u3 = wu[:, :, DK:]
    iu = iwu[:, :, DK:]
    kd = (k3 * jnp.exp(glast - gcol)).astype(BF16)
    gl = jnp.exp(glast)

    s = s_ref[...]
    for c in range(nc):
        bs = slice(c * GH, (c + 1) * GH)
        ws_qs = _bdot(wq[bs], s.astype(BF16))
        vnew = u3[bs] - ws_qs[:, :CHUNK, :]
        o = ws_qs[:, CHUNK:, :] + iu[bs]
        s = s * gl[bs] + _bdot_tn(kd[bs], vnew.astype(BF16))
        on = o * lax.rsqrt(jnp.mean(o * o, axis=-1, keepdims=True) + EPS) * norm_w
        for h in range(GH):
            o_ref[rs(c), ls(h, DV)] = (on[h] * _silu(z_ref[rs(c), ls(h, DV)])).astype(o_ref.dtype)
    s_ref[...] = s

    @pl.when(c_idx == pl.num_programs(1) - 1)
    def _():
        sfin_ref[...] = s


def _gdn_prompt(proj, betac, gc, gr, conv_w, norm_a, rows):
    t = proj.shape[0]
    nblk = t // rows
    nq = GH // (HV // HK)
    wq, wv = nq * DK, GH * DV
    kq0, kv0, kz0 = C_KA // wq, C_VA // wv, C_ZA // wv
    scr = [pltpu.VMEM((GH, DK, DV), F32),
           pltpu.VMEM((rows + 8, wq), F32), pltpu.VMEM((rows + 8, wq), F32), pltpu.VMEM((rows + 8, wv), F32),
           pltpu.VMEM((rows, wq), F32), pltpu.VMEM((rows, wq), F32), pltpu.VMEM((rows, wv), F32)]
    return pl.pallas_call(
        functools.partial(_gdn_prompt_kernel, rows=rows),
        grid=(NGRP, nblk),
        in_specs=[pl.BlockSpec((rows, wq), lambda g, c: (c, g)),
                  pl.BlockSpec((rows, wq), lambda g, c: (c, kq0 + g)),
                  pl.BlockSpec((rows, wv), lambda g, c: (c, kv0 + g)),
                  pl.BlockSpec((rows, wv), lambda g, c: (c, kz0 + g)),
                  pl.BlockSpec((None, rows, GH), lambda g, c: (g, c, 0)),
                  pl.BlockSpec((None, rows, GH), lambda g, c: (g, c, 0)),
                  pl.BlockSpec((None, rows // CHUNK, GH, CHUNK), lambda g, c: (g, c, 0, 0)),
                  pl.BlockSpec((CONV_K, wq), lambda g, c: (0, g)),
                  pl.BlockSpec((CONV_K, wq), lambda g, c: (0, kq0 + g)),
                  pl.BlockSpec((CONV_K, wv), lambda g, c: (0, kv0 + g)),
                  pl.BlockSpec((1, DV), lambda g, c: (0, 0))],
        out_specs=[pl.BlockSpec((rows, wv), lambda g, c: (c, g)),
                   pl.BlockSpec((GH, DK, DV), lambda g, c: (g, 0, 0))],
        out_shape=[jax.ShapeDtypeStruct((t, V_DIM), BF16),
                   jax.ShapeDtypeStruct((HV, DK, DV), F32)],
        scratch_shapes=scr,
        compiler_params=_cparams(("parallel", "arbitrary")),
        name="gdn_prompt",
    )(proj, proj, proj, proj, betac, gc, gr, conv_w, conv_w, conv_w, norm_a)


def _gdn_sample_kernel(q_ref, k_ref, v_ref, z_ref, beta_ref, g_ref, cq_ref, ck_ref, cv_ref,
                       cwq_ref, cwk_ref, cwv_ref, norm_ref, s0_ref, o_ref, sout_ref, *, nt):
    nq = GH // (HV // HK)

    def conv_rows(x_ref, st_ref, w_ref):
        rows = [st_ref[i:i + 1, :] for i in range(CONV_K - 1)] + [x_ref[t:t + 1, :] for t in range(nt)]
        out = []
        for t in range(nt):
            y = rows[t] * w_ref[0:1, :]
            for i in range(1, CONV_K):
                y = y + rows[t + i] * w_ref[i:i + 1, :]
            out.append(_silu(y))
        return out

    def l2n(x):
        return x * lax.rsqrt(jnp.sum(x * x, axis=-1, keepdims=True) + EPS)

    qrows = conv_rows(q_ref, cq_ref, cwq_ref)
    krows = conv_rows(k_ref, ck_ref, cwk_ref)
    vrows = conv_rows(v_ref, cv_ref, cwv_ref)
    norm_w = norm_ref[...]
    zpad = jnp.zeros((LANE - 2 * nt, DK), F32)
    for a in range(nq):
        sl = slice(a * DK, (a + 1) * DK)
        kq = [l2n(krows[t][:, sl]) for t in range(nt)] + [l2n(qrows[t][:, sl]) * (DK ** -0.5) for t in range(nt)]
        cols = jnp.concatenate(kq + [zpad], axis=0).T
        for e in range(HV // HK):
            i = a * (HV // HK) + e
            vsl = slice(i * DV, (i + 1) * DV)
            s = s0_ref[i]
            for t in range(nt):
                kcol = cols[:, t:t + 1]
                qcol = cols[:, nt + t:nt + t + 1]
                s = s * jnp.exp(g_ref[t:t + 1, i:i + 1])
                ks = jnp.sum(s * kcol, axis=0, keepdims=True)
                vn = beta_ref[t:t + 1, i:i + 1] * (vrows[t][:, vsl] - ks)
                s = s + kcol * vn
                o = jnp.sum(s * qcol, axis=0, keepdims=True)
                on = o * lax.rsqrt(jnp.mean(o * o, axis=-1, keepdims=True) + EPS) * norm_w
                o_ref[t:t + 1, vsl] = on * _silu(z_ref[t:t + 1, vsl])
            sout_ref[i] = s


def _gdn_sample(proj3, beta4, g4, conv_state, conv_w, norm_a, s0):
    nb, nt, _ = proj3.shape
    nq = GH // (HV // HK)
    wq, wv = nq * DK, GH * DV
    kq0, kv0, kz0 = C_KA // wq, C_VA // wv, C_ZA // wv
    return pl.pallas_call(
        functools.partial(_gdn_sample_kernel, nt=nt),
        grid=(nb, NGRP),
        in_specs=[pl.BlockSpec((None, nt, wq), lambda b, g: (b, 0, g)),
                  pl.BlockSpec((None, nt, wq), lambda b, g: (b, 0, kq0 + g)),
                  pl.BlockSpec((None, nt, wv), lambda b, g: (b, 0, kv0 + g)),
                  pl.BlockSpec((None, nt, wv), lambda b, g: (b, 0, kz0 + g)),
                  pl.BlockSpec((None, None, nt, GH), lambda b, g: (g, b, 0, 0)),
                  pl.BlockSpec((None, None, nt, GH), lambda b, g: (g, b, 0, 0)),
                  pl.BlockSpec((None, CONV_K - 1, wq), lambda b, g: (b, 0, g)),
                  pl.BlockSpec((None, CONV_K - 1, wq), lambda b, g: (b, 0, kq0 + g)),
                  pl.BlockSpec((None, CONV_K - 1, wv), lambda b, g: (b, 0, kv0 + g)),
                  pl.BlockSpec((CONV_K, wq), lambda b, g: (0, g)),
                  pl.BlockSpec((CONV_K, wq), lambda b, g: (0, kq0 + g)),
                  pl.BlockSpec((CONV_K, wv), lambda b, g: (0, kv0 + g)),
                  pl.BlockSpec((1, DV), lambda b, g: (0, 0)),
                  pl.BlockSpec((None, GH, DK, DV), lambda b, g: (b, g, 0, 0))],
        out_specs=[pl.BlockSpec((None, nt, wv), lambda b, g: (b, 0, g)),
                   pl.BlockSpec((None, GH, DK, DV), lambda b, g: (b, g, 0, 0))],
        out_shape=[jax.ShapeDtypeStruct((nb, nt, V_DIM), F32),
                   jax.ShapeDtypeStruct((nb, HV, DK, DV), F32)],
        compiler_params=_cparams(("parallel", "arbitrary")),
        name="gdn_sample",
    )(proj3, proj3, proj3, proj3, beta4, g4, conv_state, conv_state, conv_state,
      conv_w, conv_w, conv_w, norm_a, s0)


def _attn_prompt_kernel(q_ref, kp_ref, kc_ref, vp_ref, vc_ref, o_ref, lse_ref, *, qblk):
    i = pl.program_id(1)
    row = lax.broadcasted_iota(jnp.int32, (qblk, 2 * qblk), 0)
    col = lax.broadcasted_iota(jnp.int32, (qblk, 2 * qblk), 1)
    in_prev = jnp.logical_and(jnp.logical_and(col < qblk, col >= row), i > 0)
    valid = jnp.logical_or(in_prev, jnp.logical_and(col >= qblk, col - qblk <= row))

    def heads(ref):
        return jnp.stack([ref[:, h * DH:(h + 1) * DH] for h in range(HB)], axis=0)

    k = jnp.concatenate([heads(kp_ref), heads(kc_ref)], axis=1)
    v = jnp.concatenate([heads(vp_ref), heads(vc_ref)], axis=1)
    s = jnp.where(valid, _bdot_nt(heads(q_ref), k) * (DH ** -0.5), NEG_INF)
    m = jnp.max(s, axis=-1, keepdims=True)
    p = jnp.exp(s - m)
    l = jnp.sum(p, axis=-1, keepdims=True)
    o = _bdot(p.astype(BF16), v) / l
    lse = m + jnp.log(l)
    lse_ref[...] = jnp.zeros_like(lse_ref)
    for h in range(HB):
        o_ref[:, h * DH:(h + 1) * DH] = o[h].astype(o_ref.dtype)
        lse_ref[:, h:h + 1] = lse[h]


def _attn_prompt(qkv, qblk):
    _, dil, ts, w = qkv.shape
    nblk = ts // qblk

    def spec(kind, prev):
        if prev:
            return pl.BlockSpec((None, None, qblk, w), lambda r, i: (kind, r, jnp.maximum(i - 1, 0), 0))
        return pl.BlockSpec((None, None, qblk, w), lambda r, i: (kind, r, i, 0))

    return pl.pallas_call(
        functools.partial(_attn_prompt_kernel, qblk=qblk),
        grid=(dil, nblk),
        in_specs=[spec(0, False), spec(1, True), spec(1, False), spec(2, True), spec(2, False)],
        out_specs=[pl.BlockSpec((None, qblk, w), lambda r, i: (r, i, 0)),
                   pl.BlockSpec((None, qblk, LANE), lambda r, i: (r, i, 0))],
        out_shape=[jax.ShapeDtypeStruct((dil, ts, w), BF16),
                   jax.ShapeDtypeStruct((dil, ts, LANE), F32)],
        compiler_params=_cparams(("parallel", "arbitrary")),
        name=f"attn_prompt_d{dil}",
    )(qkv, qkv, qkv, qkv, qkv)


def _attn_sample_kernel(q_ref, kn_ref, vn_ref, b0_ref, b1_ref, b2_ref, o_ref, *, nt):
    nkeys = b0_ref.shape[0]
    width = nkeys * HB
    lane = lax.broadcasted_iota(jnp.int32, (HB, width), 1)
    sub = lax.broadcasted_iota(jnp.int32, (HB, width), 0)
    head_ok = jnp.bitwise_and(lane, HB - 1) == sub
    key_idx = lax.shift_right_logical(lane, 3)
    scale = DH ** -0.5
    for t in range(nt):
        outs, lses = [], []
        for gi in range(NG):
            q = q_ref[t, gi]
            if gi == 0:
                kmat = b0_ref[:, 0].reshape(width, DH)
                vmat = b0_ref[:, 1].reshape(width, DH)
                valid = jnp.logical_and(head_ok, key_idx >= t)
                new_rows = range(t + 1)
            else:
                bref = b1_ref if gi == 1 else b2_ref
                kmat = bref[:, t, 0].reshape(width, DH)
                vmat = bref[:, t, 1].reshape(width, DH)
                valid = head_ok
                new_rows = (t,)
            s = jnp.where(valid, _dot_nt(q.astype(BF16), kmat.astype(BF16)) * scale, NEG_INF)
            s_new = [jnp.sum(q * kn_ref[c, gi], axis=-1, keepdims=True) * scale for c in new_rows]
            m = jnp.max(s, axis=-1, keepdims=True)
            for sn in s_new:
                m = jnp.maximum(m, sn)
            p = jnp.exp(s - m)
            l = jnp.sum(p, axis=-1, keepdims=True)
            acc = _dot(p.astype(BF16), vmat.astype(BF16))
            for c, sn in zip(new_rows, s_new):
                pn = jnp.exp(sn - m)
                l = l + pn
                acc = acc + pn * vn_ref[c, gi]
            outs.append(acc / l)
            lses.append(m + jnp.log(l))
        mx = jnp.maximum(jnp.maximum(lses[0], lses[1]), lses[2])
        es = [jnp.exp(x - mx) for x in lses]
        den = es[0] + es[1] + es[2]
        o_ref[t] = (es[0] * outs[0] + es[1] * outs[1] + es[2] * outs[2]) / den


def _attn_sample(q5, k5, v5, buf128, buf512, buf2048):
    nb, nt = q5.shape[:2]
    new_spec = pl.BlockSpec((None, nt, NG, HB, DH), lambda b: (b, 0, 0, 0, 0))
    w0 = buf128.shape[1]
    v1 = buf512.reshape(nb, buf512.shape[1] // 4, 4, 2, HB, DH)
    v2 = buf2048.reshape(nb, buf2048.shape[1] // 16, 16, 2, HB, DH)
    return pl.pallas_call(
        functools.partial(_attn_sample_kernel, nt=nt),
        grid=(nb,),
        in_specs=[new_spec, new_spec, new_spec,
                  pl.BlockSpec((None, w0, 2, HB, DH), lambda b: (b, 0, 0, 0, 0)),
                  pl.BlockSpec((None, v1.shape[1], 4, 2, HB, DH), lambda b: (b, 0, 0, 0, 0, 0)),
                  pl.BlockSpec((None, v2.shape[1], nt, 2, HB, DH), lambda b: (b, 0, 0, 0, 0, 0))],
        out_specs=pl.BlockSpec((None, nt, HB, DH), lambda b: (b, 0, 0, 0)),
        out_shape=jax.ShapeDtypeStruct((nb, nt, HB, DH), F32),
        compiler_params=_cparams(("parallel",)),
        name="attn_sample",
    )(q5, k5, v5, buf128, v1, v2)


def _roll_kernel(b0_ref, b1_ref, b2_ref, n0_ref, n1_ref, n2_ref, o0_ref, o1_ref, o2_ref, *, nt):
    for b_ref, n_ref, o_ref in ((b0_ref, n0_ref, o0_ref), (b1_ref, n1_ref, o1_ref), (b2_ref, n2_ref, o2_ref)):
        w = b_ref.shape[0]

        def body(r, carry, b_ref=b_ref, o_ref=o_ref):
            o_ref[pl.ds(r * nt, nt)] = b_ref[pl.ds(r * nt + nt, nt)]
            return carry

        lax.fori_loop(0, w // nt - 1, body, 0)
        o_ref[w - nt:w] = n_ref[...]


def _roll_buffers(bufs, news):
    nb = bufs[0].shape[0]
    nt = news[0].shape[1]
    bspec = [pl.BlockSpec((None, b.shape[1], None, HB, DH), lambda b_, kv: (b_, 0, kv, 0, 0)) for b in bufs]
    nspec = [pl.BlockSpec((None, nt, None, HB, DH), lambda b_, kv: (b_, 0, kv, 0, 0)) for _ in news]
    return pl.pallas_call(
        functools.partial(_roll_kernel, nt=nt),
        grid=(nb, 2),
        in_specs=bspec + nspec,
        out_specs=bspec,
        out_shape=[jax.ShapeDtypeStruct(b.shape, b.dtype) for b in bufs],
        compiler_params=_cparams(("parallel", "arbitrary")),
        name="roll_kv",
    )(*bufs, *news)


def _mix_kernel(o0_ref, o1_ref, o2_ref, l0_ref, l1_ref, l2_ref, z_ref, ob_ref, *, tm):
    outs, lses = [], []
    for gi, (o_ref, l_ref) in enumerate(((o0_ref, l0_ref), (o1_ref, l1_ref), (o2_ref, l2_ref))):
        dil = PATTERNS[gi][1]
        o = o_ref[...].reshape(tm, HB * DH)
        l = l_ref[...].reshape(tm, LANE)
        if dil > 1:
            ri = lax.broadcasted_iota(jnp.int32, (tm, tm), 0)
            ci = lax.broadcasted_iota(jnp.int32, (tm, tm), 1)
            back = jnp.where(ri == _residue_index(ci, tm, dil), 1.0, 0.0)
            o = _dot(back.astype(BF16), o)
            l = jnp.dot(back, l, preferred_element_type=F32, precision=lax.Precision.HIGHEST)
        else:
            o = o.astype(F32)
        outs.append(o)
        lses.append(l)
    l0, l1, l2 = lses
    mx = jnp.maximum(jnp.maximum(l0, l1), l2)
    e0, e1, e2 = jnp.exp(l0 - mx), jnp.exp(l1 - mx), jnp.exp(l2 - mx)
    inv = 1.0 / (e0 + e1 + e2)
    w0, w1, w2 = e0 * inv, e1 * inv, e2 * inv
    for h in range(HB):
        sl = slice(h * DH, (h + 1) * DH)
        hs = slice(h, h + 1)
        mixed = w0[:, hs] * outs[0][:, sl] + w1[:, hs] * outs[1][:, sl] + w2[:, hs] * outs[2][:, sl]
        ob_ref[:, sl] = (mixed * _silu(z_ref[:, sl])).astype(ob_ref.dtype)


def _mix(outs, lses, zsrc, zcol, tm):
    t = zsrc.shape[0]
    w = HB * DH
    ospecs = [pl.BlockSpec((d, tm // d, w), lambda i: (0, i, 0)) for _, d in PATTERNS]
    lspecs = [pl.BlockSpec((d, tm // d, LANE), lambda i: (0, i, 0)) for _, d in PATTERNS]
    return pl.pallas_call(
        functools.partial(_mix_kernel, tm=tm),
        grid=(t // tm,),
        in_specs=ospecs + lspecs + [pl.BlockSpec((tm, w), lambda i: (i, zcol // w))],
        out_specs=pl.BlockSpec((tm, w), lambda i: (i, 0)),
        out_shape=jax.ShapeDtypeStruct((t, w), BF16),
        compiler_params=_cparams(("parallel",)),
        name="group_mix",
    )(*outs, *lses, zsrc)


def _gate_b_kernel(o_ref, z_ref, ob_ref):
    ob_ref[...] = (o_ref[...] * _silu(z_ref[...])).astype(ob_ref.dtype)


def _gate_b(o, proj):
    t, w = o.shape
    return pl.pallas_call(
        _gate_b_kernel,
        grid=(1,),
        in_specs=[pl.BlockSpec((t, w), lambda i: (0, 0)),
                  pl.BlockSpec((t, w), lambda i: (0, C_ZB // w))],
        out_specs=pl.BlockSpec((t, w), lambda i: (0, 0)),
        out_shape=jax.ShapeDtypeStruct((t, w), BF16),
        name="gate_b_sample",
    )(o, proj)


def _merge_kernel(oa_ref, ob_ref, wa_ref, wb_ref, ga_ref, gb_ref, m_ref):
    ya = _dot(oa_ref[...], wa_ref[...])
    yb = _dot(ob_ref[...], wb_ref[...])
    m_ref[...] = (jax.nn.sigmoid(ga_ref[...]) * ya + jax.nn.sigmoid(gb_ref[...]) * yb).astype(m_ref.dtype)


def _merge(oa, ob, wa, wb, gsrc, ga_col, gb_col, tm, tn):
    t = oa.shape[0]
    return pl.pallas_call(
        _merge_kernel,
        grid=(t // tm, D_MODEL // tn),
        in_specs=[pl.BlockSpec((tm, V_DIM), lambda i, j: (i, 0)),
                  pl.BlockSpec((tm, B_OUT), lambda i, j: (i, 0)),
                  pl.BlockSpec((V_DIM, tn), lambda i, j: (0, j)),
                  pl.BlockSpec((B_OUT, tn), lambda i, j: (0, j)),
                  pl.BlockSpec((tm, tn), lambda i, j: (i, ga_col // tn + j)),
                  pl.BlockSpec((tm, tn), lambda i, j: (i, gb_col // tn + j))],
        out_specs=pl.BlockSpec((tm, tn), lambda i, j: (i, j)),
        out_shape=jax.ShapeDtypeStruct((t, D_MODEL), BF16),
        compiler_params=_cparams(("parallel", "arbitrary")),
        name="merge",
    )(oa, ob, wa, wb, gsrc, gsrc)


def _out_kernel(m_ref, x_ref, w_ref, ln_ref, y_ref):
    h = x_ref[...] + _dot(m_ref[...], w_ref[...])
    y_ref[...] = _rms_rows(h, ln_ref[...])


def _out(merged, x, w_out, ln_f, tm):
    t = x.shape[0]
    return pl.pallas_call(
        _out_kernel,
        grid=(t // tm,),
        in_specs=[pl.BlockSpec((tm, D_MODEL), lambda i: (i, 0)),
                  pl.BlockSpec((tm, D_MODEL), lambda i: (i, 0)),
                  pl.BlockSpec((D_MODEL, D_MODEL), lambda i: (0, 0)),
                  pl.BlockSpec((1, D_MODEL), lambda i: (0, 0))],
        out_specs=pl.BlockSpec((tm, D_MODEL), lambda i: (i, 0)),
        out_shape=jax.ShapeDtypeStruct((t, D_MODEL), F32),
        compiler_params=_cparams(("parallel",)),
        name="out_proj",
    )(merged, x, w_out, ln_f)


def _prep_weights(w_in, a_log, dt_bias):
    parts = [w_in[:, SPLIT_OFFS[i]:SPLIT_OFFS[i + 1]] for i in range(len(SPLIT_SIZES))]
    q_a, k_a, v_a, z_a, b_a, a_a, q_b, k_b, v_b, z_b, g_a, g_b = parts
    w_main = jnp.concatenate([q_a, k_a, v_a, z_a, q_b, k_b, v_b, z_b, g_a, g_b], axis=1).astype(BF16)
    pad = jnp.zeros((w_in.shape[0], LANE - HV), w_in.dtype)
    w_ba = jnp.concatenate([b_a, pad, a_a, pad], axis=1).astype(BF16)
    alog = jnp.pad(a_log, (0, LANE - HV)).reshape(1, LANE)
    dtb = jnp.pad(dt_bias, (0, LANE - HV)).reshape(1, LANE)
    return w_main, w_ba, alog, dtb


def _pick(n, pref):
    for c in pref:
        if n % c == 0:
            return c
    return n


def _tail(merged_in, x, w_out, ln_f):
    tm = _pick(x.shape[0], (512, 256, 128))
    return _out(merged_in, x, w_out, ln_f, tm)


def _prompt(x, ln_in, w_main, w_ba, alog, dtb, conv_w, norm_a, wa, wb, w_out, ln_f):
    t = x.shape[0]
    tm = _pick(t, (1024, 512, 256))
    proj_a = _inproj(x, ln_in, w_main, 0, C_QB, tm, 1024)
    proj_c = _inproj(x, ln_in, w_main, C_ZB, N_MAIN - C_ZB, tm, 1024)
    betac, gc, gr = _ba_chunked(x, ln_in, w_ba, alog, dtb, _pick(t, (256,)))
    oa, s_fin = _gdn_prompt(proj_a, betac, gc, gr, conv_w, norm_a, _pick(t, (256,)))
    outs, lses, kvs = [], [], []
    for gi, (win, dil) in enumerate(PATTERNS):
        qkv = _inproj_res(x, ln_in, w_main, gi, dil, tm)
        o, l = _attn_prompt(qkv, 128)
        outs.append(o)
        lses.append(l)
        n = min(win, t)
        last = qkv[1:, :, (t - n) // dil:, :].astype(F32)
        kvs.append(last.transpose(2, 1, 0, 3).reshape(n, 2, HB, DH))
    ob = _mix(outs, lses, proj_c, 0, _pick(t, (256,)))
    merged = _merge(oa, ob, wa, wb, proj_c, C_GA - C_ZB, C_GB - C_ZB, _pick(t, (512, 256)), 512)
    y = _tail(merged, x, w_out, ln_f)
    conv_new = proj_a[t - (CONV_K - 1):, :CONV_DIM]
    return y, kvs, s_fin, conv_new


def _sample(xs, bufs, s0, conv0, ln_in, w_main, w_ba, alog, dtb, conv_w, norm_a, wa, wb, w_out, ln_f):
    nb, nt, d = xs.shape
    x = xs.reshape(nb * nt, d)
    proj = _inproj(x, ln_in, w_main, 0, N_MAIN, nb * nt, 1024)
    beta, g = _ba_plain(x, ln_in, w_ba, alog, dtb)
    beta4 = beta[:, :HV].reshape(nb, nt, NGRP, GH).transpose(2, 0, 1, 3)
    g4 = g[:, :HV].reshape(nb, nt, NGRP, GH).transpose(2, 0, 1, 3)
    proj3 = proj.reshape(nb, nt, N_MAIN)
    oa, s_new = _gdn_sample(proj3, beta4, g4, conv0, conv_w, norm_a, s0)
    oa = oa.reshape(nb * nt, V_DIM).astype(BF16)
    q5 = proj[:, C_QB:C_QB + B_DIM].reshape(nb, nt, NG, HB, DH)
    k5 = proj[:, C_KB:C_KB + B_DIM].reshape(nb, nt, NG, HB, DH)
    v5 = proj[:, C_VB:C_VB + B_DIM].reshape(nb, nt, NG, HB, DH)
    o_b = _attn_sample(q5, k5, v5, *bufs).reshape(nb * nt, B_OUT)
    ob = _gate_b(o_b, proj)
    merged = _merge(oa, ob, wa, wb, proj, C_GA, C_GB, nb * nt, 512)
    y = _tail(merged, x, w_out, ln_f).reshape(nb, nt, d)
    kv5 = jnp.stack([k5, v5], axis=2)
    news = [kv5[:, :, :, gi] for gi in range(NG)]
    new_bufs = _roll_buffers(bufs, news)
    conv_new = proj3[:, nt - (CONV_K - 1):, :CONV_DIM]
    return y, new_bufs, s_new, conv_new


def kernel(x_prompt, x_sample, cache_kv_w128, cache_kv_w512, cache_kv_w2048, state_delta, state_conv,
           ln_in, w_in, conv_w, a_log, dt_bias, norm_a, w_proj_a, w_proj_b, w_out, ln_f):
    w_main, w_ba, alog, dtb = _prep_weights(w_in[0], a_log[0], dt_bias[0])
    shared = (ln_in, w_main, w_ba, alog, dtb, conv_w[0], norm_a,
              w_proj_a[0].astype(BF16), w_proj_b[0].astype(BF16), w_out[0].astype(BF16), ln_f.reshape(1, -1))
    y_p, kv_p, d_p, c_p = _prompt(x_prompt[0], *shared)
    y_s, kv_s, d_s, c_s = _sample(x_sample, (cache_kv_w128[0], cache_kv_w512[0], cache_kv_w2048[0]),
                                  state_delta[0], state_conv[0], *shared)
    return (y_p[None], y_s,
            kv_p[0][None, None], kv_p[1][None, None], kv_p[2][None, None], d_p[None, None], c_p[None, None],
            kv_s[0][None], kv_s[1][None], kv_s[2][None], d_s[None], c_s[None])
```

```python
import functools

import jax
import jax.numpy as jnp
import numpy as np
from jax import lax
from jax.experimental import pallas as pl
from jax.experimental.pallas import tpu as pltpu

F32 = jnp.float32
BF16 = jnp.bfloat16

D_MODEL = 2048
HK, HV, DK, DV = 16, 32, 128, 128
CONV_K = 4
CHUNK = 64
HB, DH = 8, 128
PATTERNS = ((128, 1), (512, 4), (2048, 16))
NG = len(PATTERNS)
EPS = 1e-6
NEG_INF = -1e30

QK_DIM = HK * DK
V_DIM = HV * DV
CONV_DIM = 2 * QK_DIM + V_DIM
B_DIM = NG * HB * DH
B_OUT = HB * DH
SPLIT_SIZES = (QK_DIM, QK_DIM, V_DIM, V_DIM, HV, HV, B_DIM, B_DIM, B_DIM, B_OUT, D_MODEL, D_MODEL)
SPLIT_OFFS = tuple(int(s) for s in np.cumsum((0,) + SPLIT_SIZES))

C_QA, C_KA, C_VA = 0, QK_DIM, 2 * QK_DIM
C_ZA = CONV_DIM
N_GDN = C_ZA + V_DIM
R_QB, R_KB, R_VB = 0, B_DIM, 2 * B_DIM
R_ZB = 3 * B_DIM
R_GA = R_ZB + B_OUT
R_GB = R_GA + D_MODEL
N_REST = R_GB + D_MODEL

LANE = 128
GH = 8
PERM_ROWS = 256
NGRP = HV // GH
NQ = GH // (HV // HK)
PAIR = (HV // HK) * CHUNK
VMEM_LIMIT = 56 * 1024 * 1024


def _cparams(sem):
    return pltpu.CompilerParams(dimension_semantics=sem, vmem_limit_bytes=VMEM_LIMIT)


def _silu(x):
    return x * jax.nn.sigmoid(x)


def _softplus(x):
    return jnp.maximum(x, 0.0) + jnp.log1p(jnp.exp(-jnp.abs(x)))


def _rms_rows(x, w):
    r = lax.rsqrt(jnp.mean(x * x, axis=-1, keepdims=True) + EPS)
    return x * r * w


def _dot(a, b):
    return jnp.dot(a, b, preferred_element_type=F32)


def _dot_nt(a, b):
    return lax.dot_general(a, b, (((1,), (1,)), ((), ())), preferred_element_type=F32)


def _dot_tn(a, b):
    return lax.dot_general(a, b, (((0,), (0,)), ((), ())), preferred_element_type=F32)


def _bdot(a, b):
    return lax.dot_general(a, b, (((2,), (1,)), ((0,), (0,))), preferred_element_type=F32)


def _bdot_nt(a, b):
    return lax.dot_general(a, b, (((2,), (2,)), ((0,), (0,))), preferred_element_type=F32)


def _bdot_tn(a, b):
    return lax.dot_general(a, b, (((1,), (1,)), ((0,), (0,))), preferred_element_type=F32)


def _inproj_kernel(x_ref, ln_ref, w_ref, o_ref, xn_ref):
    @pl.when(pl.program_id(1) == 0)
    def _():
        xn_ref[...] = _rms_rows(x_ref[...], ln_ref[...]).astype(BF16)

    o_ref[...] = _dot(xn_ref[...], w_ref[...]).astype(o_ref.dtype)


def _inproj(x, ln, w, col0, ncols, tm, tn, dtype=F32):
    m, d = x.shape
    c0 = col0 // tn
    return pl.pallas_call(
        _inproj_kernel,
        grid=(m // tm, ncols // tn),
        in_specs=[pl.BlockSpec((tm, d), lambda i, j: (i, 0)),
                  pl.BlockSpec((1, d), lambda i, j: (0, 0)),
                  pl.BlockSpec((d, tn), lambda i, j: (0, c0 + j))],
        out_specs=pl.BlockSpec((tm, tn), lambda i, j: (i, j)),
        out_shape=jax.ShapeDtypeStruct((m, ncols), dtype),
        scratch_shapes=[pltpu.VMEM((tm, d), BF16)],
        compiler_params=_cparams(("parallel", "arbitrary")),
        name="inproj",
    )(x, ln, w)


def _inproj_conv_kernel(x_ref, ln_ref, w_ref, cw_ref, o_ref, tail_ref, xn_ref, carry_ref,
                        *, tm, tn, l2norm, nscaled):
    i = pl.program_id(0)
    j = pl.program_id(1)
    keep = 8

    @pl.when(j == 0)
    def _():
        xn_ref[...] = _rms_rows(x_ref[...], ln_ref[...]).astype(BF16)

    @pl.when(i == 0)
    def _():
        carry_ref[j] = jnp.zeros((keep, tn), F32)

    acc = _dot(xn_ref[...], w_ref[...])
    tail_ref[...] = acc[tm - keep:tm]
    ext = jnp.concatenate([carry_ref[j], acc], axis=0)
    carry_ref[j] = acc[tm - keep:tm]
    y = acc * cw_ref[CONV_K - 1:CONV_K, :]
    for k in range(1, CONV_K):
        y = y + pltpu.roll(ext, k, axis=0)[keep:] * cw_ref[CONV_K - 1 - k:CONV_K - k, :]
    y = _silu(y)
    if l2norm:
        scale = jnp.where(j < nscaled, DK ** -0.5, 1.0)
        for h in range(tn // DK):
            yh = y[:, h * DK:(h + 1) * DK]
            r = lax.rsqrt(jnp.sum(yh * yh, axis=-1, keepdims=True) + EPS) * scale
            o_ref[:, h * DK:(h + 1) * DK] = (yh * r).astype(o_ref.dtype)
    else:
        o_ref[...] = y.astype(o_ref.dtype)


def _inproj_conv(x, ln, w, conv_w, col0, ncols, tm, tn, l2norm, nscaled):
    m, d = x.shape
    c0 = col0 // tn
    nt = ncols // tn
    return pl.pallas_call(
        functools.partial(_inproj_conv_kernel, tm=tm, tn=tn, l2norm=l2norm, nscaled=nscaled),
        grid=(m // tm, nt),
        in_specs=[pl.BlockSpec((tm, d), lambda i, j: (i, 0)),
                  pl.BlockSpec((1, d), lambda i, j: (0, 0)),
                  pl.BlockSpec((d, tn), lambda i, j: (0, c0 + j)),
                  pl.BlockSpec((CONV_K, tn), lambda i, j: (0, c0 + j))],
        out_specs=[pl.BlockSpec((tm, tn), lambda i, j: (i, j)),
                   pl.BlockSpec((None, 8, tn), lambda i, j: (i, 0, j))],
        out_shape=[jax.ShapeDtypeStruct((m, ncols), BF16),
                   jax.ShapeDtypeStruct((m // tm, 8, ncols), F32)],
        scratch_shapes=[pltpu.VMEM((tm, d), BF16), pltpu.VMEM((nt, 8, tn), F32)],
        compiler_params=_cparams(("arbitrary", "arbitrary")),
        name="inproj_conv_qk" if l2norm else "inproj_conv_v",
    )(x, ln, w, conv_w)


def _residue_index(i, n, dil):
    sub = n // dil
    return jnp.bitwise_and(i, sub - 1) * dil + lax.shift_right_logical(i, sub.bit_length() - 1)


def _perm_to_residue(n, dil):
    i = lax.broadcasted_iota(jnp.int32, (n, n), 0)
    j = lax.broadcasted_iota(jnp.int32, (n, n), 1)
    return j == _residue_index(i, n, dil)


def _inproj_res_kernel(x_ref, ln_ref, w_ref, o_ref, xn_ref, *, dil, tm):
    pg = min(tm, PERM_ROWS)
    sub = pg // dil

    @pl.when(pl.program_id(1) == 0)
    def _():
        xn = _rms_rows(x_ref[...], ln_ref[...]).astype(BF16)
        if dil == 1:
            xn_ref[...] = xn
        else:
            perm = jnp.where(_perm_to_residue(pg, dil), 1.0, 0.0).astype(BF16)
            for g in range(tm // pg):
                xn_ref[g * pg:(g + 1) * pg, :] = _dot(perm, xn[g * pg:(g + 1) * pg]).astype(BF16)

    acc = _dot(xn_ref[...], w_ref[...])
    for g in range(tm // pg):
        for r in range(dil):
            o_ref[r, g * sub:(g + 1) * sub, :] = acc[g * pg + r * sub:g * pg + (r + 1) * sub].astype(o_ref.dtype)


def _inproj_res(x, ln, w, gi, dil, tm):
    t, d = x.shape
    wd = HB * DH
    c0 = R_QB // wd + gi
    return pl.pallas_call(
        functools.partial(_inproj_res_kernel, dil=dil, tm=tm),
        grid=(t // tm, 3),
        in_specs=[pl.BlockSpec((tm, d), lambda i, j: (i, 0)),
                  pl.BlockSpec((1, d), lambda i, j: (0, 0)),
                  pl.BlockSpec((d, wd), lambda i, j: (0, c0 + NG * j))],
        out_specs=pl.BlockSpec((None, dil, tm // dil, wd), lambda i, j: (j, 0, i, 0)),
        out_shape=jax.ShapeDtypeStruct((3, dil, t // dil, wd), BF16),
        scratch_shapes=[pltpu.VMEM((tm, d), BF16)],
        compiler_params=_cparams(("parallel", "arbitrary")),
        name=f"inproj_attn_d{dil}",
    )(x, ln, w)


def _beta_g(x_ref, ln_ref, w_ref, alog_ref, dtb_ref):
    xn = _rms_rows(x_ref[...], ln_ref[...]).astype(BF16)
    p = _dot(xn, w_ref[...])
    beta = jax.nn.sigmoid(p[:, :LANE])
    g = -jnp.exp(alog_ref[...]) * _softplus(p[:, LANE:] + dtb_ref[...])
    return beta, g


def _ba_chunked_kernel(x_ref, ln_ref, w_ref, alog_ref, dtb_ref, betac_ref, gc_ref, gr_ref, *, tm):
    beta, g = _beta_g(x_ref, ln_ref, w_ref, alog_ref, dtb_ref)
    row = lax.broadcasted_iota(jnp.int32, (tm, tm), 0)
    col = lax.broadcasted_iota(jnp.int32, (tm, tm), 1)
    same = lax.shift_right_logical(row, 6) == lax.shift_right_logical(col, 6)
    tri = jnp.where(jnp.logical_and(same, col <= row), 1.0, 0.0).astype(F32)
    gcum = jnp.dot(tri, g, preferred_element_type=F32, precision=lax.Precision.HIGHEST)
    gcum_t = gcum.T
    rep = HV // HK
    for a in range(NGRP):
        betac_ref[a] = beta[:, a * GH:(a + 1) * GH]
        gc_ref[a] = gcum[:, a * GH:(a + 1) * GH]
        for cc in range(tm // CHUNK):
            cs = slice(cc * CHUNK, (cc + 1) * CHUNK)
            for q in range(GH // rep):
                h0 = a * GH + q * rep
                gr_ref[a, cc, q:q + 1, :] = jnp.concatenate(
                    [gcum_t[h0 + e:h0 + e + 1, cs] for e in range(rep)], axis=1)


def _ba_chunked(x, ln, w_ba, alog, dtb, tm):
    m, d = x.shape
    return pl.pallas_call(
        functools.partial(_ba_chunked_kernel, tm=tm),
        grid=(m // tm,),
        in_specs=[pl.BlockSpec((tm, d), lambda i: (i, 0)),
                  pl.BlockSpec((1, d), lambda i: (0, 0)),
                  pl.BlockSpec((d, 2 * LANE), lambda i: (0, 0)),
                  pl.BlockSpec((1, LANE), lambda i: (0, 0)),
                  pl.BlockSpec((1, LANE), lambda i: (0, 0))],
        out_specs=[pl.BlockSpec((NGRP, tm, GH), lambda i: (0, i, 0)),
                   pl.BlockSpec((NGRP, tm, GH), lambda i: (0, i, 0)),
                   pl.BlockSpec((NGRP, tm // CHUNK, NQ, PAIR), lambda i: (0, i, 0, 0))],
        out_shape=[jax.ShapeDtypeStruct((NGRP, m, GH), F32),
                   jax.ShapeDtypeStruct((NGRP, m, GH), F32),
                   jax.ShapeDtypeStruct((NGRP, m // CHUNK, NQ, PAIR), F32)],
        compiler_params=_cparams(("parallel",)),
        name="beta_decay_prompt",
    )(x, ln, w_ba, alog, dtb)


def _ba_plain_kernel(x_ref, ln_ref, w_ref, alog_ref, dtb_ref, beta_ref, g_ref):
    beta, g = _beta_g(x_ref, ln_ref, w_ref, alog_ref, dtb_ref)
    beta_ref[...] = beta
    g_ref[...] = g


def _ba_plain(x, ln, w_ba, alog, dtb):
    m, d = x.shape
    return pl.pallas_call(
        _ba_plain_kernel,
        out_shape=[jax.ShapeDtypeStruct((m, LANE), F32), jax.ShapeDtypeStruct((m, LANE), F32)],
        compiler_params=pltpu.CompilerParams(vmem_limit_bytes=VMEM_LIMIT),
        name="beta_decay_sample",
    )(x, ln, w_ba, alog, dtb)


def _gdn_prompt_kernel(q_ref, k_ref, v_ref, z_ref, betac_ref, gc_ref, gr_ref, norm_ref,
                       o_ref, sfin_ref, s_ref, *, rows):
    c_idx = pl.program_id(1)
    nc = rows // CHUNK
    rep = HV // HK

    @pl.when(c_idx == 0)
    def _():
        s_ref[...] = jnp.zeros_like(s_ref)

    row = lax.broadcasted_iota(jnp.int32, (CHUNK, PAIR), 0)
    lane = lax.broadcasted_iota(jnp.int32, (CHUNK, PAIR), 1)
    lloc = jnp.bitwise_and(lane, CHUNK - 1)
    causal = lloc <= row
    strict = lloc < row
    eye = jnp.where(lloc == row, 1.0, 0.0).astype(F32)
    side = [lax.shift_right_logical(lane, 6) == e for e in range(rep)]
    norm_w = norm_ref[...]

    def rs(c):
        return slice(c * CHUNK, (c + 1) * CHUNK)

    def ls(h, w):
        return slice(h * w, (h + 1) * w)

    def stack_ca(fn):
        return jnp.stack([fn(c, a) for c in range(nc) for a in range(NQ)], axis=0)

    def side_by_side(ref, c, a):
        out = jnp.broadcast_to(ref[rs(c), a * rep:a * rep + 1], (CHUNK, PAIR))
        for e in range(1, rep):
            out = jnp.where(side[e], jnp.broadcast_to(ref[rs(c), a * rep + e:a * rep + e + 1], (CHUNK, PAIR)), out)
        return out

    def block_diag(x):
        zero = jnp.zeros_like(x)
        return jnp.concatenate([jnp.where(side[e], x, zero) for e in range(rep)], axis=1)

    def block_rows(xs):
        zero = jnp.zeros_like(xs[0])
        return jnp.concatenate(
            [jnp.concatenate([xs[e] if f == e else zero for f in range(rep)], axis=2) for e in range(rep)], axis=1)

    kb = stack_ca(lambda c, a: k_ref[rs(c), ls(a, DK)])
    qb = stack_ca(lambda c, a: q_ref[rs(c), ls(a, DK)])
    kq = _bdot_nt(jnp.concatenate([kb, qb], axis=1), jnp.concatenate([kb] * rep, axis=1))
    gcol = stack_ca(lambda c, a: side_by_side(gc_ref, c, a))
    bcol = stack_ca(lambda c, a: side_by_side(betac_ref, c, a))
    grow = stack_ca(lambda c, a: gr_ref[c, a:a + 1, :])
    decay = jnp.exp(jnp.where(causal, gcol - grow, NEG_INF))
    lmat = jnp.where(strict, kq[:, :CHUNK, :] * bcol * decay, 0.0)
    intra = jnp.where(causal, kq[:, CHUNK:, :] * decay, 0.0).astype(BF16)
    lb = lmat.astype(BF16)
    pw = _bdot(lb, block_diag(lb))
    tinv = eye - lmat
    span = 4
    while span < CHUNK:
        pb = pw.astype(BF16)
        both = _bdot(jnp.concatenate([tinv.astype(BF16), pb], axis=1), block_diag(pb))
        tinv = tinv + both[:, :CHUNK, :]
        pw = both[:, CHUNK:, :]
        span *= 2
    tinv = (tinv + _bdot(tinv.astype(BF16), block_diag(pw.astype(BF16)))).astype(BF16)

    kf = kb.astype(F32)
    qf = qb.astype(F32)
    rhs = []
    for e in range(rep):
        bk = stack_ca(lambda c, a: jnp.broadcast_to(
            betac_ref[rs(c), a * rep + e:a * rep + e + 1] * jnp.exp(gc_ref[rs(c), a * rep + e:a * rep + e + 1]),
            (CHUNK, DK)))
        bv = stack_ca(lambda c, a: jnp.broadcast_to(betac_ref[rs(c), a * rep + e:a * rep + e + 1], (CHUNK, DV)))
        v_e = stack_ca(lambda c, a: v_ref[rs(c), ls(a * rep + e, DV)]).astype(F32)
        rhs.append(jnp.concatenate([kf * bk, v_e * bv], axis=2).astype(BF16))
    wu = _bdot(tinv, block_rows(rhs))
    wub = wu.astype(BF16)
    wd = DK + DV
    iwu = _bdot(intra, block_rows([wub[:, :, e * wd:(e + 1) * wd] for e in range(rep)]))

    s = s_ref[...]
    for c in range(nc):
        wq, u3, iu, kd, gl = [], [], [], [], []
        for h in range(GH):
            a, e = divmod(h, rep)
            b = c * NQ + a
            gcol_h = gc_ref[rs(c), h:h + 1]
            glast = gcol_h[CHUNK - 1:CHUNK, :]
            w_h = wu[b, :, e * wd:e * wd + DK]
            qp = qf[b] * jnp.exp(gcol_h) - iwu[b, :, e * wd:e * wd + DK]
            wq.append(jnp.concatenate([w_h, qp], axis=0).astype(BF16))
            u3.append(wu[b, :, e * wd + DK:(e + 1) * wd])
            iu.append(iwu[b, :, e * wd + DK:(e + 1) * wd])
            kd.append((kf[b] * jnp.exp(glast - gcol_h)).astype(BF16))
            gl.append(jnp.exp(glast))
        ws_qs = _bdot(jnp.stack(wq, axis=0), s.astype(BF16))
        vnew = jnp.stack(u3, axis=0) - ws_qs[:, :CHUNK, :]
        o = ws_qs[:, CHUNK:, :] + jnp.stack(iu, axis=0)
        s = s * jnp.stack(gl, axis=0) + _bdot_tn(jnp.stack(kd, axis=0), vnew.astype(BF16))
        on = o * lax.rsqrt(jnp.mean(o * o, axis=-1, keepdims=True) + EPS) * norm_w
        for h in range(GH):
            o_ref[rs(c), ls(h, DV)] = (on[h] * _silu(z_ref[rs(c), ls(h, DV)].astype(F32))).astype(o_ref.dtype)
    s_ref[...] = s

    @pl.when(c_idx == pl.num_programs(1) - 1)
    def _():
        sfin_ref[...] = s


def _gdn_prompt(qk, v, z, betac, gc, gr, norm_a, rows):
    t = qk.shape[0]
    nblk = t // rows
    wq, wv = NQ * DK, GH * DV
    kq0 = QK_DIM // wq
    return pl.pallas_call(
        functools.partial(_gdn_prompt_kernel, rows=rows),
        grid=(NGRP, nblk),
        in_specs=[pl.BlockSpec((rows, wq), lambda g, c: (c, g)),
                  pl.BlockSpec((rows, wq), lambda g, c: (c, kq0 + g)),
                  pl.BlockSpec((rows, wv), lambda g, c: (c, g)),
                  pl.BlockSpec((rows, wv), lambda g, c: (c, g)),
                  pl.BlockSpec((None, rows, GH), lambda g, c: (g, c, 0)),
                  pl.BlockSpec((None, rows, GH), lambda g, c: (g, c, 0)),
                  pl.BlockSpec((None, rows // CHUNK, NQ, PAIR), lambda g, c: (g, c, 0, 0)),
                  pl.BlockSpec((1, DV), lambda g, c: (0, 0))],
        out_specs=[pl.BlockSpec((rows, wv), lambda g, c: (c, g)),
                   pl.BlockSpec((GH, DK, DV), lambda g, c: (g, 0, 0))],
        out_shape=[jax.ShapeDtypeStruct((t, V_DIM), BF16),
                   jax.ShapeDtypeStruct((HV, DK, DV), F32)],
        scratch_shapes=[pltpu.VMEM((GH, DK, DV), F32)],
        compiler_params=_cparams(("parallel", "arbitrary")),
        name="gdn_prompt",
    )(qk, qk, v, z, betac, gc, gr, norm_a)


def _gdn_sample_kernel(q_ref, k_ref, v_ref, z_ref, beta_ref, g_ref, cq_ref, ck_ref, cv_ref,
                       cwq_ref, cwk_ref, cwv_ref, norm_ref, s0_ref, o_ref, sout_ref, *, nt):
    nq = GH // (HV // HK)

    def conv_rows(x_ref, st_ref, w_ref):
        rows = [st_ref[i:i + 1, :] for i in range(CONV_K - 1)] + [x_ref[t:t + 1, :] for t in range(nt)]
        out = []
        for t in range(nt):
            y = rows[t] * w_ref[0:1, :]
            for i in range(1, CONV_K):
                y = y + rows[t + i] * w_ref[i:i + 1, :]
            out.append(_silu(y))
        return out

    def l2n(x):
        return x * lax.rsqrt(jnp.sum(x * x, axis=-1, keepdims=True) + EPS)

    qrows = conv_rows(q_ref, cq_ref, cwq_ref)
    krows = conv_rows(k_ref, ck_ref, cwk_ref)
    vrows = conv_rows(v_ref, cv_ref, cwv_ref)
    norm_w = norm_ref[...]
    zpad = jnp.zeros((LANE - 2 * nt, DK), F32)
    for a in range(nq):
        sl = slice(a * DK, (a + 1) * DK)
        kq = [l2n(krows[t][:, sl]) for t in range(nt)] + [l2n(qrows[t][:, sl]) * (DK ** -0.5) for t in range(nt)]
        cols = jnp.concatenate(kq + [zpad], axis=0).T
        for e in range(HV // HK):
            i = a * (HV // HK) + e
            vsl = slice(i * DV, (i + 1) * DV)
            s = s0_ref[i]
            for t in range(nt):
                kcol = cols[:, t:t + 1]
                qcol = cols[:, nt + t:nt + t + 1]
                s = s * jnp.exp(g_ref[t:t + 1, i:i + 1])
                ks = jnp.sum(s * kcol, axis=0, keepdims=True)
                vn = beta_ref[t:t + 1, i:i + 1] * (vrows[t][:, vsl] - ks)
                s = s + kcol * vn
                o = jnp.sum(s * qcol, axis=0, keepdims=True)
                on = o * lax.rsqrt(jnp.mean(o * o, axis=-1, keepdims=True) + EPS) * norm_w
                o_ref[t:t + 1, vsl] = on * _silu(z_ref[t:t + 1, vsl])
            sout_ref[i] = s


def _gdn_sample(proj3, beta4, g4, conv_state, conv_w, norm_a, s0):
    nb, nt, _ = proj3.shape
    nq = GH // (HV // HK)
    wq, wv = nq * DK, GH * DV
    kq0, kv0, kz0 = C_KA // wq, C_VA // wv, C_ZA // wv
    return pl.pallas_call(
        functools.partial(_gdn_sample_kernel, nt=nt),
        grid=(nb, NGRP),
        in_specs=[pl.BlockSpec((None, nt, wq), lambda b, g: (b, 0, g)),
                  pl.BlockSpec((None, nt, wq), lambda b, g: (b, 0, kq0 + g)),
                  pl.BlockSpec((None, nt, wv), lambda b, g: (b, 0, kv0 + g)),
                  pl.BlockSpec((None, nt, wv), lambda b, g: (b, 0, kz0 + g)),
                  pl.BlockSpec((None, None, nt, GH), lambda b, g: (g, b, 0, 0)),
                  pl.BlockSpec((None, None, nt, GH), lambda b, g: (g, b, 0, 0)),
                  pl.BlockSpec((None, CONV_K - 1, wq), lambda b, g: (b, 0, g)),
                  pl.BlockSpec((None, CONV_K - 1, wq), lambda b, g: (b, 0, kq0 + g)),
                  pl.BlockSpec((None, CONV_K - 1, wv), lambda b, g: (b, 0, kv0 + g)),
                  pl.BlockSpec((CONV_K, wq), lambda b, g: (0, g)),
                  pl.BlockSpec((CONV_K, wq), lambda b, g: (0, kq0 + g)),
                  pl.BlockSpec((CONV_K, wv), lambda b, g: (0, kv0 + g)),
                  pl.BlockSpec((1, DV), lambda b, g: (0, 0)),
                  pl.BlockSpec((None, GH, DK, DV), lambda b, g: (b, g, 0, 0))],
        out_specs=[pl.BlockSpec((None, nt, wv), lambda b, g: (b, 0, g)),
                   pl.BlockSpec((None, GH, DK, DV), lambda b, g: (b, g, 0, 0))],
        out_shape=[jax.ShapeDtypeStruct((nb, nt, V_DIM), F32),
                   jax.ShapeDtypeStruct((nb, HV, DK, DV), F32)],
        compiler_params=_cparams(("parallel", "arbitrary")),
        name="gdn_sample",
    )(proj3, proj3, proj3, proj3, beta4, g4, conv_state, conv_state, conv_state,
      conv_w, conv_w, conv_w, norm_a, s0)


def _attn_prompt_kernel(q_ref, kp_ref, kc_ref, vp_ref, vc_ref, o_ref, lse_ref, *, qblk):
    i = pl.program_id(1)
    row = lax.broadcasted_iota(jnp.int32, (qblk, 2 * qblk), 0)
    col = lax.broadcasted_iota(jnp.int32, (qblk, 2 * qblk), 1)
    in_prev = jnp.logical_and(jnp.logical_and(col < qblk, col >= row), i > 0)
    valid = jnp.logical_or(in_prev, jnp.logical_and(col >= qblk, col - qblk <= row))

    def heads(ref):
        return jnp.stack([ref[:, h * DH:(h + 1) * DH] for h in range(HB)], axis=0)

    k = jnp.concatenate([heads(kp_ref), heads(kc_ref)], axis=1)
    v = jnp.concatenate([heads(vp_ref), heads(vc_ref)], axis=1)
    s = jnp.where(valid, _bdot_nt(heads(q_ref), k) * (DH ** -0.5), NEG_INF)
    m = jnp.max(s, axis=-1, keepdims=True)
    p = jnp.exp(s - m)
    l = jnp.sum(p, axis=-1, keepdims=True)
    o = _bdot(p.astype(BF16), v) / l
    lse = m + jnp.log(l)
    lse_ref[...] = jnp.zeros_like(lse_ref)
    for h in range(HB):
        o_ref[:, h * DH:(h + 1) * DH] = o[h].astype(o_ref.dtype)
        lse_ref[:, h:h + 1] = lse[h]


def _attn_prompt(qkv, qblk):
    _, dil, ts, w = qkv.shape
    nblk = ts // qblk

    def spec(kind, prev):
        if prev:
            return pl.BlockSpec((None, None, qblk, w), lambda r, i: (kind, r, jnp.maximum(i - 1, 0), 0))
        return pl.BlockSpec((None, None, qblk, w), lambda r, i: (kind, r, i, 0))

    return pl.pallas_call(
        functools.partial(_attn_prompt_kernel, qblk=qblk),
        grid=(dil, nblk),
        in_specs=[spec(0, False), spec(1, True), spec(1, False), spec(2, True), spec(2, False)],
        out_specs=[pl.BlockSpec((None, qblk, w), lambda r, i: (r, i, 0)),
                   pl.BlockSpec((None, qblk, LANE), lambda r, i: (r, i, 0))],
        out_shape=[jax.ShapeDtypeStruct((dil, ts, w), BF16),
                   jax.ShapeDtypeStruct((dil, ts, LANE), F32)],
        compiler_params=_cparams(("parallel", "arbitrary")),
        name=f"attn_prompt_d{dil}",
    )(qkv, qkv, qkv, qkv, qkv)


def _attn_sample_kernel(q_ref, kn_ref, vn_ref, b0_ref, b1_ref, b2_ref, o_ref, *, nt):
    nkeys = b0_ref.shape[0]
    width = nkeys * HB
    lane = lax.broadcasted_iota(jnp.int32, (HB, width), 1)
    sub = lax.broadcasted_iota(jnp.int32, (HB, width), 0)
    head_ok = jnp.bitwise_and(lane, HB - 1) == sub
    key_idx = lax.shift_right_logical(lane, 3)
    scale = DH ** -0.5
    for t in range(nt):
        outs, lses = [], []
        for gi in range(NG):
            q = q_ref[t, gi]
            if gi == 0:
                kmat = b0_ref[:, 0].reshape(width, DH)
                vmat = b0_ref[:, 1].reshape(width, DH)
                valid = jnp.logical_and(head_ok, key_idx >= t)
                new_rows = range(t + 1)
            else:
                bref = b1_ref if gi == 1 else b2_ref
                kmat = bref[:, t, 0].reshape(width, DH)
                vmat = bref[:, t, 1].reshape(width, DH)
                valid = head_ok
                new_rows = (t,)
            s = jnp.where(valid, _dot_nt(q.astype(BF16), kmat.astype(BF16)) * scale, NEG_INF)
            s_new = [jnp.sum(q * kn_ref[c, gi], axis=-1, keepdims=True) * scale for c in new_rows]
            m = jnp.max(s, axis=-1, keepdims=True)
            for sn in s_new:
                m = jnp.maximum(m, sn)
            p = jnp.exp(s - m)
            l = jnp.sum(p, axis=-1, keepdims=True)
            acc = _dot(p.astype(BF16), vmat.astype(BF16))
            for c, sn in zip(new_rows, s_new):
                pn = jnp.exp(sn - m)
                l = l + pn
                acc = acc + pn * vn_ref[c, gi]
            outs.append(acc / l)
            lses.append(m + jnp.log(l))
        mx = jnp.maximum(jnp.maximum(lses[0], lses[1]), lses[2])
        es = [jnp.exp(x - mx) for x in lses]
        den = es[0] + es[1] + es[2]
        o_ref[t] = (es[0] * outs[0] + es[1] * outs[1] + es[2] * outs[2]) / den


def _attn_sample(q5, k5, v5, buf128, buf512, buf2048):
    nb, nt = q5.shape[:2]
    new_spec = pl.BlockSpec((None, nt, NG, HB, DH), lambda b: (b, 0, 0, 0, 0))
    w0 = buf128.shape[1]
    v1 = buf512.reshape(nb, buf512.shape[1] // 4, 4, 2, HB, DH)
    v2 = buf2048.reshape(nb, buf2048.shape[1] // 16, 16, 2, HB, DH)
    return pl.pallas_call(
        functools.partial(_attn_sample_kernel, nt=nt),
        grid=(nb,),
        in_specs=[new_spec, new_spec, new_spec,
                  pl.BlockSpec((None, w0, 2, HB, DH), lambda b: (b, 0, 0, 0, 0)),
                  pl.BlockSpec((None, v1.shape[1], 4, 2, HB, DH), lambda b: (b, 0, 0, 0, 0, 0)),
                  pl.BlockSpec((None, v2.shape[1], nt, 2, HB, DH), lambda b: (b, 0, 0, 0, 0, 0))],
        out_specs=pl.BlockSpec((None, nt, HB, DH), lambda b: (b, 0, 0, 0)),
        out_shape=jax.ShapeDtypeStruct((nb, nt, HB, DH), F32),
        compiler_params=_cparams(("parallel",)),
        name="attn_sample",
    )(q5, k5, v5, buf128, v1, v2)


def _roll_kernel(b0_ref, b1_ref, b2_ref, n0_ref, n1_ref, n2_ref, o0_ref, o1_ref, o2_ref, *, nt):
    for b_ref, n_ref, o_ref in ((b0_ref, n0_ref, o0_ref), (b1_ref, n1_ref, o1_ref), (b2_ref, n2_ref, o2_ref)):
        w = b_ref.shape[0]

        def body(r, carry, b_ref=b_ref, o_ref=o_ref):
            o_ref[pl.ds(r * nt, nt)] = b_ref[pl.ds(r * nt + nt, nt)]
            return carry

        lax.fori_loop(0, w // nt - 1, body, 0)
        o_ref[w - nt:w] = n_ref[...]


def _roll_buffers(bufs, news):
    nb = bufs[0].shape[0]
    nt = news[0].shape[1]
    bspec = [pl.BlockSpec((None, b.shape[1], None, HB, DH), lambda b_, kv: (b_, 0, kv, 0, 0)) for b in bufs]
    nspec = [pl.BlockSpec((None, nt, None, HB, DH), lambda b_, kv: (b_, 0, kv, 0, 0)) for _ in news]
    return pl.pallas_call(
        functools.partial(_roll_kernel, nt=nt),
        grid=(nb, 2),
        in_specs=bspec + nspec,
        out_specs=bspec,
        out_shape=[jax.ShapeDtypeStruct(b.shape, b.dtype) for b in bufs],
        compiler_params=_cparams(("parallel", "arbitrary")),
        name="roll_kv",
    )(*bufs, *news)


def _mix_kernel(o0_ref, o1_ref, o2_ref, l0_ref, l1_ref, l2_ref, z_ref, ob_ref, *, tm):
    outs, lses = [], []
    for gi, (o_ref, l_ref) in enumerate(((o0_ref, l0_ref), (o1_ref, l1_ref), (o2_ref, l2_ref))):
        dil = PATTERNS[gi][1]
        o = o_ref[...].reshape(tm, HB * DH)
        l = l_ref[...].reshape(tm, LANE)
        if dil > 1:
            ri = lax.broadcasted_iota(jnp.int32, (tm, tm), 0)
            ci = lax.broadcasted_iota(jnp.int32, (tm, tm), 1)
            back = jnp.where(ri == _residue_index(ci, tm, dil), 1.0, 0.0)
            o = _dot(back.astype(BF16), o)
            l = jnp.dot(back, l, preferred_element_type=F32, precision=lax.Precision.HIGHEST)
        else:
            o = o.astype(F32)
        outs.append(o)
        lses.append(l)
    l0, l1, l2 = lses
    mx = jnp.maximum(jnp.maximum(l0, l1), l2)
    e0, e1, e2 = jnp.exp(l0 - mx), jnp.exp(l1 - mx), jnp.exp(l2 - mx)
    inv = 1.0 / (e0 + e1 + e2)
    w0, w1, w2 = e0 * inv, e1 * inv, e2 * inv
    for h in range(HB):
        sl = slice(h * DH, (h + 1) * DH)
        hs = slice(h, h + 1)
        mixed = w0[:, hs] * outs[0][:, sl] + w1[:, hs] * outs[1][:, sl] + w2[:, hs] * outs[2][:, sl]
        ob_ref[:, sl] = (mixed * _silu(z_ref[:, sl])).astype(ob_ref.dtype)


def _mix(outs, lses, zsrc, zcol, tm):
    t = zsrc.shape[0]
    w = HB * DH
    ospecs = [pl.BlockSpec((d, tm // d, w), lambda i: (0, i, 0)) for _, d in PATTERNS]
    lspecs = [pl.BlockSpec((d, tm // d, LANE), lambda i: (0, i, 0)) for _, d in PATTERNS]
    return pl.pallas_call(
        functools.partial(_mix_kernel, tm=tm),
        grid=(t // tm,),
        in_specs=ospecs + lspecs + [pl.BlockSpec((tm, w), lambda i: (i, zcol // w))],
        out_specs=pl.BlockSpec((tm, w), lambda i: (i, 0)),
        out_shape=jax.ShapeDtypeStruct((t, w), BF16),
        compiler_params=_cparams(("parallel",)),
        name="group_mix",
    )(*outs, *lses, zsrc)


def _gate_b_kernel(o_ref, z_ref, ob_ref):
    ob_ref[...] = (o_ref[...] * _silu(z_ref[...])).astype(ob_ref.dtype)


def _gate_b(o, proj):
    t, w = o.shape
    return pl.pallas_call(
        _gate_b_kernel,
        grid=(1,),
        in_specs=[pl.BlockSpec((t, w), lambda i: (0, 0)),
                  pl.BlockSpec((t, w), lambda i: (0, (N_GDN + R_ZB) // w))],
        out_specs=pl.BlockSpec((t, w), lambda i: (0, 0)),
        out_shape=jax.ShapeDtypeStruct((t, w), BF16),
        name="gate_b_sample",
    )(o, proj)


def _merge_kernel(oa_ref, ob_ref, wa_ref, wb_ref, ga_ref, gb_ref, m_ref):
    ya = _dot(oa_ref[...], wa_ref[...])
    yb = _dot(ob_ref[...], wb_ref[...])
    m_ref[...] = (jax.nn.sigmoid(ga_ref[...]) * ya + jax.nn.sigmoid(gb_ref[...]) * yb).astype(m_ref.dtype)


def _merge(oa, ob, wa, wb, gsrc, ga_col, gb_col, tm, tn):
    t = oa.shape[0]
    return pl.pallas_call(
        _merge_kernel,
        grid=(t // tm, D_MODEL // tn),
        in_specs=[pl.BlockSpec((tm, V_DIM), lambda i, j: (i, 0)),
                  pl.BlockSpec((tm, B_OUT), lambda i, j: (i, 0)),
                  pl.BlockSpec((V_DIM, tn), lambda i, j: (0, j)),
                  pl.BlockSpec((B_OUT, tn), lambda i, j: (0, j)),
                  pl.BlockSpec((tm, tn), lambda i, j: (i, ga_col // tn + j)),
                  pl.BlockSpec((tm, tn), lambda i, j: (i, gb_col // tn + j))],
        out_specs=pl.BlockSpec((tm, tn), lambda i, j: (i, j)),
        out_shape=jax.ShapeDtypeStruct((t, D_MODEL), BF16),
        compiler_params=_cparams(("parallel", "arbitrary")),
        name="merge",
    )(oa, ob, wa, wb, gsrc, gsrc)


def _out_kernel(m_ref, x_ref, w_ref, ln_ref, y_ref):
    h = x_ref[...] + _dot(m_ref[...], w_ref[...])
    y_ref[...] = _rms_rows(h, ln_ref[...])


def _out(merged, x, w_out, ln_f, tm):
    t = x.shape[0]
    return pl.pallas_call(
        _out_kernel,
        grid=(t // tm,),
        in_specs=[pl.BlockSpec((tm, D_MODEL), lambda i: (i, 0)),
                  pl.BlockSpec((tm, D_MODEL), lambda i: (i, 0)),
                  pl.BlockSpec((D_MODEL, D_MODEL), lambda i: (0, 0)),
                  pl.BlockSpec((1, D_MODEL), lambda i: (0, 0))],
        out_specs=pl.BlockSpec((tm, D_MODEL), lambda i: (i, 0)),
        out_shape=jax.ShapeDtypeStruct((t, D_MODEL), F32),
        compiler_params=_cparams(("parallel",)),
        name="out_proj",
    )(merged, x, w_out, ln_f)


def _prep_weights(w_in, a_log, dt_bias):
    w_gdn = w_in[:, :SPLIT_OFFS[4]].astype(BF16)
    w_rest = w_in[:, SPLIT_OFFS[6]:].astype(BF16)
    pad = jnp.zeros((w_in.shape[0], LANE - HV), w_in.dtype)
    w_ba = jnp.concatenate([w_in[:, SPLIT_OFFS[4]:SPLIT_OFFS[5]], pad,
                            w_in[:, SPLIT_OFFS[5]:SPLIT_OFFS[6]], pad], axis=1).astype(BF16)
    alog = jnp.pad(a_log, (0, LANE - HV)).reshape(1, LANE)
    dtb = jnp.pad(dt_bias, (0, LANE - HV)).reshape(1, LANE)
    return w_gdn, w_rest, w_ba, alog, dtb


def _pick(n, pref):
    for c in pref:
        if n % c == 0:
            return c
    return n


def _tail(merged_in, x, w_out, ln_f):
    tm = _pick(x.shape[0], (512, 256, 128))
    return _out(merged_in, x, w_out, ln_f, tm)


def _prompt(x, ln_in, w_gdn, w_rest, w_ba, alog, dtb, conv_w, norm_a, wa, wb, w_out, ln_f):
    t = x.shape[0]
    tm = _pick(t, (1024, 512, 256))
    qk_a, tail_qk = _inproj_conv(x, ln_in, w_gdn, conv_w, C_QA, 2 * QK_DIM, tm, 1024, True, QK_DIM // 1024)
    v_a, tail_v = _inproj_conv(x, ln_in, w_gdn, conv_w, C_VA, V_DIM, tm, 1024, False, 0)
    z_a = _inproj(x, ln_in, w_gdn, C_ZA, V_DIM, tm, 1024, BF16)
    proj_c = _inproj(x, ln_in, w_rest, R_ZB, N_REST - R_ZB, tm, 1024)
    betac, gc, gr = _ba_chunked(x, ln_in, w_ba, alog, dtb, _pick(t, (256,)))
    oa, s_fin = _gdn_prompt(qk_a, v_a, z_a, betac, gc, gr, norm_a, _pick(t, (256,)))
    outs, lses, kvs = [], [], []
    for gi, (win, dil) in enumerate(PATTERNS):
        qkv = _inproj_res(x, ln_in, w_rest, gi, dil, tm)
        o, l = _attn_prompt(qkv, 128)
        outs.append(o)
        lses.append(l)
        n = min(win, t)
        last = qkv[1:, :, (t - n) // dil:, :].astype(F32)
        kvs.append(last.transpose(2, 1, 0, 3).reshape(n, 2, HB, DH))
    ob = _mix(outs, lses, proj_c, 0, _pick(t, (256,)))
    merged = _merge(oa, ob, wa, wb, proj_c, R_GA - R_ZB, R_GB - R_ZB, _pick(t, (512, 256)), 512)
    y = _tail(merged, x, w_out, ln_f)
    conv_new = jnp.concatenate([tail_qk[-1], tail_v[-1]], axis=1)[8 - (CONV_K - 1):]
    return y, kvs, s_fin, conv_new


def _sample(xs, bufs, s0, conv0, ln_in, w_gdn, w_rest, w_ba, alog, dtb, conv_w, norm_a, wa, wb, w_out, ln_f):
    nb, nt, d = xs.shape
    x = xs.reshape(nb * nt, d)
    proj = jnp.concatenate([_inproj(x, ln_in, w_gdn, 0, N_GDN, nb * nt, 1024),
                            _inproj(x, ln_in, w_rest, 0, N_REST, nb * nt, 1024)], axis=1)
    beta, g = _ba_plain(x, ln_in, w_ba, alog, dtb)
    beta4 = beta[:, :HV].reshape(nb, nt, NGRP, GH).transpose(2, 0, 1, 3)
    g4 = g[:, :HV].reshape(nb, nt, NGRP, GH).transpose(2, 0, 1, 3)
    proj3 = proj.reshape(nb, nt, N_GDN + N_REST)
    oa, s_new = _gdn_sample(proj3, beta4, g4, conv0, conv_w, norm_a, s0)
    oa = oa.reshape(nb * nt, V_DIM).astype(BF16)
    q5 = proj[:, N_GDN + R_QB:N_GDN + R_KB].reshape(nb, nt, NG, HB, DH)
    k5 = proj[:, N_GDN + R_KB:N_GDN + R_VB].reshape(nb, nt, NG, HB, DH)
    v5 = proj[:, N_GDN + R_VB:N_GDN + R_ZB].reshape(nb, nt, NG, HB, DH)
    o_b = _attn_sample(q5, k5, v5, *bufs).reshape(nb * nt, B_OUT)
    ob = _gate_b(o_b, proj)
    merged = _merge(oa, ob, wa, wb, proj, N_GDN + R_GA, N_GDN + R_GB, nb * nt, 512)
    y = _tail(merged, x, w_out, ln_f).reshape(nb, nt, d)
    kv5 = jnp.stack([k5, v5], axis=2)
    news = [kv5[:, :, :, gi] for gi in range(NG)]
    new_bufs = _roll_buffers(bufs, news)
    conv_new = proj3[:, nt - (CONV_K - 1):, :CONV_DIM]
    return y, new_bufs, s_new, conv_new


def kernel(x_prompt, x_sample, cache_kv_w128, cache_kv_w512, cache_kv_w2048, state_delta, state_conv,
           ln_in, w_in, conv_w, a_log, dt_bias, norm_a, w_proj_a, w_proj_b, w_out, ln_f):
    w_gdn, w_rest, w_ba, alog, dtb = _prep_weights(w_in[0], a_log[0], dt_bias[0])
    shared = (ln_in, w_gdn, w_rest, w_ba, alog, dtb, conv_w[0], norm_a,
              w_proj_a[0].astype(BF16), w_proj_b[0].astype(BF16), w_out[0].astype(BF16), ln_f.reshape(1, -1))
    y_p, kv_p, d_p, c_p = _prompt(x_prompt[0], *shared)
    y_s, kv_s, d_s, c_s = _sample(x_sample, (cache_kv_w128[0], cache_kv_w512[0], cache_kv_w2048[0]),
                                  state_delta[0], state_conv[0], *shared)
    return (y_p[None], y_s,
            kv_p[0][None, None], kv_p[1][None, None], kv_p[2][None, None], d_p[None, None], c_p[None, None],
            kv_s[0][None], kv_s[1][None], kv_s[2][None], d_s[None], c_s[None])
```
